```python
import math
import jax, jax.numpy as jnp
from jax import lax
import numpy as np

D_MODEL = 1024
BATCH = 8
SEQ = 4096
DEPTH = 2

CONV_K = 4
CHUNK = 64
NORM_EPS = 1e-6
DT_MIN = 1e-3
DT_MAX = 1e-1
S5_GROUP = 16
S5_STATE = 64
S5_WIDTH = 3 * D_MODEL // 8
S5_GROUPS = S5_WIDTH // S5_GROUP
GDN_HEADS = 4
GDN_DK = 128
GDN_DV = 128
GDN_QK = GDN_HEADS * GDN_DK
GDN_WIDTH = GDN_HEADS * GDN_DV
SSD_HEAD_DIM = 64
SSD_WIDTH = D_MODEL // 2
SSD_HEADS = SSD_WIDTH // SSD_HEAD_DIM
SSD_GROUPS = 2
SSD_HPG = SSD_HEADS // SSD_GROUPS
SSD_STATE = 64
SSD_BC = SSD_GROUPS * SSD_STATE
LRU_WIDTH = D_MODEL // 2
LRU_BLOCK = 64
LRU_BLOCKS = LRU_WIDTH // LRU_BLOCK
LRU_C = 8.0
N_BRANCH = 4
BRANCH_WIDTHS = (S5_WIDTH, GDN_WIDTH, SSD_WIDTH, LRU_WIDTH)
MIX_WIDTH = sum(BRANCH_WIDTHS)
CONV_SILU_WIDTHS = (GDN_QK, GDN_QK, GDN_WIDTH, SSD_WIDTH, SSD_BC, SSD_BC)
CONV_SILU_CH = sum(CONV_SILU_WIDTHS)
CONV_CH = CONV_SILU_CH + LRU_WIDTH
REST_WIDTHS = (S5_WIDTH, GDN_HEADS, GDN_HEADS, GDN_WIDTH, SSD_WIDTH, SSD_HEADS, LRU_WIDTH, N_BRANCH * D_MODEL)
IN_COLS = CONV_CH + sum(REST_WIDTHS)
FFN_HIDDEN = -(-8 * D_MODEL // (3 * 256)) * 256

kernel_name = "adaln_hybrid_s5_gdn_ssd_rglru_trunk"

F32 = jnp.float32


def _rms(x):
    x32 = x.astype(F32)
    return x32 * lax.rsqrt(jnp.mean(x32 * x32, axis=-1, keepdims=True) + NORM_EPS)


def _split(t, widths, axis=-1):
    idx, acc = [], 0
    for w in widths[:-1]:
        acc += w
        idx.append(acc)
    return jnp.split(t, idx, axis=axis)


def _causal_conv(x, w, b):
    k_taps, s = w.shape[0], x.shape[1]
    xp = jnp.pad(x, ((0, 0), (k_taps - 1, 0), (0, 0)))
    y = b + xp[:, 0:s] * w[0]
    for k in range(1, k_taps):
        y = y + xp[:, k:k + s] * w[k]
    return y


def _causal_masks(n):
    pos = jnp.arange(n)
    return pos[:, None] >= pos[None, :], pos[:, None] > pos[None, :]


def _segment_decay(cs, incl):
    diff = cs[..., :, None] - cs[..., None, :]
    return jnp.exp(jnp.where(incl, diff, -jnp.inf))


def _linear_combine(left, right):
    a_l, b_l = left
    a_r, b_r = right
    return a_r * a_l, a_r * b_l + b_r


def _s5_branch(u, lam_re, lam_im, log_dt, b_re, b_im, c_re, c_im, d, glu_w, glu_b):
    bsz, s, _ = u.shape
    ug = u.astype(F32).reshape(bsz, s, S5_GROUPS, S5_GROUP)
    lam = lax.complex(lam_re.astype(F32), lam_im.astype(F32))
    dt = jnp.exp(log_dt.astype(F32))[:, None]
    lam_bar = jnp.exp(lam * dt)
    b_mat = lax.complex(b_re.astype(F32), b_im.astype(F32))
    b_bar = ((lam_bar - 1.0) / lam)[..., None] * b_mat
    bu = jnp.einsum('bsgh,gph->bsgp', ug.astype(jnp.complex64), b_bar)
    a = jnp.broadcast_to(lam_bar, (1, s) + lam_bar.shape)
    _, states = lax.associative_scan(_linear_combine, (a, bu), axis=1)
    c_mat = lax.complex(c_re.astype(F32), c_im.astype(F32))
    y = jnp.real(jnp.einsum('bsgp,ghp->bsgh', states, c_mat)) + d.astype(F32).reshape(S5_GROUPS, S5_GROUP) * ug
    y = jax.nn.gelu(y.reshape(bsz, s, S5_WIDTH))
    return y * jax.nn.sigmoid(y @ glu_w.astype(F32) + glu_b.astype(F32))


def _chunked_gated_delta(q, k, v, g, beta):
    bsz, s, h, dk = q.shape
    dv = v.shape[-1]
    nc = s // CHUNK

    def blocks(t):
        return jnp.moveaxis(t.reshape((bsz, nc, CHUNK, h) + t.shape[3:]), 3, 2)

    q, k, v, g, beta = blocks(q), blocks(k), blocks(v), blocks(g), blocks(beta)
    g = jnp.cumsum(g, axis=-1)
    incl, strict = _causal_masks(CHUNK)
    decay = _segment_decay(g, incl)
    kb = k * beta[..., None]
    a_strict = jnp.where(strict, jnp.einsum('bnhcd,bnhsd->bnhcs', kb, k) * decay, 0.0)
    t_mat = a_strict + jnp.eye(CHUNK, dtype=F32)
    rhs = jnp.concatenate([kb * jnp.exp(g)[..., None], v * beta[..., None]], axis=-1)
    sol = lax.linalg.triangular_solve(t_mat, rhs, left_side=True, lower=True, unit_diagonal=True)
    w, u = sol[..., :dk], sol[..., dk:]
    attn = jnp.einsum('bnhcd,bnhsd->bnhcs', q, k) * decay
    g_last = g[..., -1]
    q_dec = q * jnp.exp(g)[..., None]
    k_dec = k * jnp.exp(g_last[..., None] - g)[..., None]

    def step(state, inp):
        w_c, u_c, q_c, k_c, attn_c, gl_c = inp
        v_new = u_c - jnp.einsum('bhcd,bhde->bhce', w_c, state)
        o_c = jnp.einsum('bhcd,bhde->bhce', q_c, state) + jnp.einsum('bhcs,bhse->bhce', attn_c, v_new)
        state = state * jnp.exp(gl_c)[..., None, None] + jnp.einsum('bhcd,bhce->bhde', k_c, v_new)
        return state, o_c

    xs = tuple(jnp.moveaxis(t, 1, 0) for t in (w, u, q_dec, k_dec, attn, g_last))
    _, o = lax.scan(step, jnp.zeros((bsz, h, dk, dv), F32), xs)
    return jnp.transpose(o, (1, 0, 3, 2, 4)).reshape(bsz, s, h, dv)


def _gdn_branch(q, k, v, b_raw, a_raw, z, a_log, dt_bias, norm_w):
    bsz, s, _ = q.shape
    q = q.astype(F32).reshape(bsz, s, GDN_HEADS, GDN_DK)
    k = k.astype(F32).reshape(bsz, s, GDN_HEADS, GDN_DK)
    v = v.astype(F32).reshape(bsz, s, GDN_HEADS, GDN_DV)
    q = q * lax.rsqrt(jnp.sum(q * q, axis=-1, keepdims=True) + NORM_EPS) * (GDN_DK ** -0.5)
    k = k * lax.rsqrt(jnp.sum(k * k, axis=-1, keepdims=True) + NORM_EPS)
    beta = jax.nn.sigmoid(b_raw.astype(F32))
    g = -jnp.exp(a_log.astype(F32)) * jax.nn.softplus(a_raw.astype(F32) + dt_bias.astype(F32))
    o = _chunked_gated_delta(q, k, v, g, beta)
    o = _rms(o) * norm_w.astype(F32) * jax.nn.silu(z.astype(F32).reshape(bsz, s, GDN_HEADS, GDN_DV))
    return o.reshape(bsz, s, GDN_WIDTH)


def _chunked_ssd(xdt, la, bm, cm):
    bsz, s = xdt.shape[:2]
    nc = s // CHUNK
    xdt = xdt.reshape(bsz, nc, CHUNK, SSD_GROUPS, SSD_HPG, SSD_HEAD_DIM)
    bm = bm.reshape(bsz, nc, CHUNK, SSD_GROUPS, SSD_STATE)
    cm = cm.reshape(bsz, nc, CHUNK, SSD_GROUPS, SSD_STATE)
    cs = jnp.cumsum(jnp.moveaxis(la.reshape(bsz, nc, CHUNK, SSD_GROUPS, SSD_HPG), 2, -1), axis=-1)
    incl, _ = _causal_masks(CHUNK)
    decay = _segment_decay(cs, incl)
    cb = jnp.einsum('bzlgn,bzmgn->bzglm', cm, bm)
    y_diag = jnp.einsum('bzglm,bzgjlm,bzmgjp->bzlgjp', cb, decay, xdt)
    to_end = jnp.exp(cs[..., -1:] - cs)
    chunk_states = jnp.einsum('bzmgn,bzgjm,bzmgjp->bzgjpn', bm, to_end, xdt)
    chunk_decay = jnp.exp(cs[..., -1])

    def step(state, inp):
        st_c, dec_c = inp
        return state * dec_c[..., None, None] + st_c, state

    state0 = jnp.zeros((bsz, SSD_GROUPS, SSD_HPG, SSD_HEAD_DIM, SSD_STATE), F32)
    _, prev = lax.scan(step, state0, (jnp.moveaxis(chunk_states, 1, 0), jnp.moveaxis(chunk_decay, 1, 0)))
    prev = jnp.moveaxis(prev, 0, 1)
    y_off = jnp.einsum('bzlgn,bzgjpn,bzgjl->bzlgjp', cm, prev, jnp.exp(cs))
    return (y_diag + y_off).reshape(bsz, s, SSD_GROUPS, SSD_HPG, SSD_HEAD_DIM)


def _ssd_branch(xs, bs, cs, z, dt_raw, a_log, dt_bias, d, norm_w):
    bsz, s, _ = xs.shape
    x = xs.astype(F32).reshape(bsz, s, SSD_GROUPS, SSD_HPG, SSD_HEAD_DIM)
    bm = bs.astype(F32).reshape(bsz, s, SSD_GROUPS, SSD_STATE)
    cm = cs.astype(F32).reshape(bsz, s, SSD_GROUPS, SSD_STATE)
    dt = jax.nn.softplus(dt_raw.astype(F32) + dt_bias.astype(F32)).reshape(bsz, s, SSD_GROUPS, SSD_HPG)
    a = -jnp.exp(a_log.astype(F32)).reshape(SSD_GROUPS, SSD_HPG)
    y = _chunked_ssd(x * dt[..., None], dt * a, bm, cm)
    y = y + d.astype(F32).reshape(SSD_GROUPS, SSD_HPG)[..., None] * x
    y = y.reshape(bsz, s, SSD_WIDTH) * jax.nn.silu(z.astype(F32))
    return _rms(y) * norm_w.astype(F32)


def _rglru_branch(x, gate, lam, wr, br, wi, bi):
    bsz, s, _ = x.shape
    x = x.astype(F32)
    xb = x.reshape(bsz, s, LRU_BLOCKS, LRU_BLOCK)
    r = jax.nn.sigmoid(jnp.einsum('bsnd,nde->bsne', xb, wr.astype(F32)).reshape(bsz, s, LRU_WIDTH) + br.astype(F32))
    i = jax.nn.sigmoid(jnp.einsum('bsnd,nde->bsne', xb, wi.astype(F32)).reshape(bsz, s, LRU_WIDTH) + bi.astype(F32))
    log_a = -LRU_C * r * jax.nn.softplus(-lam.astype(F32))
    a = jnp.exp(log_a)
    mult = jnp.sqrt(-jnp.expm1(2.0 * log_a))
    mult = jnp.where((jnp.arange(s) == 0)[None, :, None], 1.0, mult)
    _, h = lax.associative_scan(_linear_combine, (a, mult * i * x), axis=1)
    return h * jax.nn.gelu(gate.astype(F32))


def _swiglu(h, w13, w2):
    a, b = jnp.split(h @ w13, 2, axis=-1)
    return (jax.nn.silu(a) * b) @ w2


def setup_inputs(seed: int = 0) -> dict:
    key = jax.random.key(seed)
    ks = iter(jax.random.split(key, 48))
    L = DEPTH

    def nrm(shape, scale):
        return scale * jax.random.normal(next(ks), shape, F32)

    def unif(shape, lo, hi):
        return jax.random.uniform(next(ks), shape, F32, lo, hi)

    def dt_bias(shape):
        dt = jnp.exp(unif(shape, math.log(DT_MIN), math.log(DT_MAX)))
        return dt + jnp.log(-jnp.expm1(-dt))

    a0 = unif((L, LRU_WIDTH), 0.9, 0.999) ** (1.0 / LRU_C)
    return {
        "x": nrm((BATCH, SEQ, D_MODEL), 1.0),
        "c": nrm((BATCH, D_MODEL), 1.0),
        "ln_mix_g": 1.0 + nrm((L, D_MODEL), 0.02),
        "ln_ffn_g": 1.0 + nrm((L, D_MODEL), 0.02),
        "ln_final_g": 1.0 + nrm((D_MODEL,), 0.02),
        "ada_w": nrm((L, D_MODEL, 6 * D_MODEL), D_MODEL ** -0.5),
        "ada_b": nrm((L, 6 * D_MODEL), 0.01),
        "w_in": nrm((L, D_MODEL, IN_COLS), D_MODEL ** -0.5),
        "conv_w": nrm((L, CONV_K, CONV_CH), CONV_K ** -0.5),
        "conv_b": nrm((L, CONV_CH), 0.01),
        "s5_lambda_re": -0.5 + nrm((L, S5_GROUPS, S5_STATE), 0.01),
        "s5_lambda_im": jnp.pi * jnp.arange(S5_STATE, dtype=F32) + nrm((L, S5_GROUPS, S5_STATE), 0.01),
        "s5_log_dt": unif((L, S5_GROUPS), math.log(DT_MIN), math.log(DT_MAX)),
        "s5_b_re": nrm((L, S5_GROUPS, S5_STATE, S5_GROUP), (2 * S5_GROUP) ** -0.5),
        "s5_b_im": nrm((L, S5_GROUPS, S5_STATE, S5_GROUP), (2 * S5_GROUP) ** -0.5),
        "s5_c_re": nrm((L, S5_GROUPS, S5_GROUP, S5_STATE), (2 * S5_STATE) ** -0.5),
        "s5_c_im": nrm((L, S5_GROUPS, S5_GROUP, S5_STATE), (2 * S5_STATE) ** -0.5),
        "s5_d": nrm((L, S5_WIDTH), 1.0),
        "s5_glu_w": nrm((L, S5_WIDTH, S5_WIDTH), S5_WIDTH ** -0.5),
        "s5_glu_b": nrm((L, S5_WIDTH), 0.01),
        "gdn_a_log": jnp.log(unif((L, GDN_HEADS), 1.0, 16.0)),
        "gdn_dt_bias": dt_bias((L, GDN_HEADS)),
        "gdn_norm_w": 1.0 + nrm((L, GDN_DV), 0.02),
        "ssd_a_log": jnp.log(unif((L, SSD_HEADS), 1.0, 16.0)),
        "ssd_dt_bias": dt_bias((L, SSD_HEADS)),
        "ssd_d": 1.0 + nrm((L, SSD_HEADS), 0.1),
        "ssd_norm_w": 1.0 + nrm((L, SSD_WIDTH), 0.02),
        "lru_lambda": jnp.log(a0) - jnp.log1p(-a0),
        "lru_wr": nrm((L, LRU_BLOCKS, LRU_BLOCK, LRU_BLOCK), LRU_BLOCK ** -0.5),
        "lru_br": nrm((L, LRU_WIDTH), 0.01),
        "lru_wi": nrm((L, LRU_BLOCKS, LRU_BLOCK, LRU_BLOCK), LRU_BLOCK ** -0.5),
        "lru_bi": nrm((L, LRU_WIDTH), 0.01),
        "w_branch": nrm((L, MIX_WIDTH, D_MODEL), GDN_WIDTH ** -0.5),
        "w_out": nrm((L, D_MODEL, D_MODEL), D_MODEL ** -0.5),
        "ffn_w13": nrm((L, D_MODEL, 2 * FFN_HIDDEN), D_MODEL ** -0.5),
        "ffn_w2": nrm((L, FFN_HIDDEN, D_MODEL), FFN_HIDDEN ** -0.5),
    }


def reference(x, c, ln_mix_g, ln_ffn_g, ln_final_g, ada_w, ada_b, w_in, conv_w, conv_b,
              s5_lambda_re, s5_lambda_im, s5_log_dt, s5_b_re, s5_b_im, s5_c_re, s5_c_im,
              s5_d, s5_glu_w, s5_glu_b, gdn_a_log, gdn_dt_bias, gdn_norm_w,
              ssd_a_log, ssd_dt_bias, ssd_d, ssd_norm_w,
              lru_lambda, lru_wr, lru_br, lru_wi, lru_bi,
              w_branch, w_out, ffn_w13, ffn_w2):
    bsz, seq_len, _ = x.shape
    cond = jax.nn.silu(c)
    for l in range(DEPTH):
        mod = (cond @ ada_w[l] + ada_b[l])[:, None, :]
        sh_m, sc_m, gt_m, sh_f, sc_f, gt_f = jnp.split(mod, 6, axis=-1)

        h = (_rms(x) * ln_mix_g[l] * (1.0 + sc_m) + sh_m).astype(x.dtype)
        proj = h @ w_in[l]
        xc = _causal_conv(proj[..., :CONV_CH], conv_w[l], conv_b[l])
        q, k, v, x_ssd, b_ssd, c_ssd = _split(jax.nn.silu(xc[..., :CONV_SILU_CH]), CONV_SILU_WIDTHS)
        x_lru = xc[..., CONV_SILU_CH:]
        u_s5, b_gdn, a_gdn, z_gdn, z_ssd, dt_ssd, g_lru, g_merge = _split(proj[..., CONV_CH:], REST_WIDTHS)

        y_a = _s5_branch(u_s5, s5_lambda_re[l], s5_lambda_im[l], s5_log_dt[l], s5_b_re[l], s5_b_im[l],
                         s5_c_re[l], s5_c_im[l], s5_d[l], s5_glu_w[l], s5_glu_b[l])
        y_b = _gdn_branch(q, k, v, b_gdn, a_gdn, z_gdn, gdn_a_log[l], gdn_dt_bias[l], gdn_norm_w[l])
        y_c = _ssd_branch(x_ssd, b_ssd, c_ssd, z_ssd, dt_ssd, ssd_a_log[l], ssd_dt_bias[l], ssd_d[l], ssd_norm_w[l])
        y_d = _rglru_branch(x_lru, g_lru, lru_lambda[l], lru_wr[l], lru_br[l], lru_wi[l], lru_bi[l])

        gates = jax.nn.sigmoid(g_merge.astype(F32)).reshape(bsz, seq_len, N_BRANCH, D_MODEL)
        rows = _split(w_branch[l], BRANCH_WIDTHS, axis=0)
        merged = (gates[:, :, 0] * (y_a @ rows[0]) + gates[:, :, 1] * (y_b @ rows[1])
                  + gates[:, :, 2] * (y_c @ rows[2]) + gates[:, :, 3] * (y_d @ rows[3]))
        x = x + gt_m * (merged @ w_out[l]).astype(x.dtype)

        h = (_rms(x) * ln_ffn_g[l] * (1.0 + sc_f) + sh_f).astype(x.dtype)
        x = x + gt_f * _swiglu(h, ffn_w13[l], ffn_w2[l]).astype(x.dtype)
    return (_rms(x) * ln_final_g).astype(x.dtype)
```

```python
import functools

import jax
import jax.numpy as jnp
from jax import lax
from jax.experimental import pallas as pl
from jax.experimental.pallas import tpu as pltpu

F32 = jnp.float32
BF16 = jnp.bfloat16

D_MODEL = 1024
DEPTH = 2
CONV_K = 4
NORM_EPS = 1e-6
S5_GROUP, S5_STATE, S5_WIDTH, S5_GROUPS = 16, 64, 384, 24
S5_HALF = S5_GROUPS * S5_STATE
GDN_HEADS, GDN_DK, GDN_DV = 4, 128, 128
GDN_QK = GDN_HEADS * GDN_DK
GDN_WIDTH = GDN_HEADS * GDN_DV
SSD_HEAD_DIM, SSD_WIDTH, SSD_HEADS, SSD_GROUPS, SSD_HPG, SSD_STATE = 64, 512, 8, 2, 4, 64
SSD_BC = SSD_GROUPS * SSD_STATE
LRU_WIDTH, LRU_BLOCK, LRU_BLOCKS, LRU_C = 512, 64, 8, 8.0
N_BRANCH = 4
FFN_HIDDEN = 2816

COL_QKV, COL_ZGDN, COL_XSSD, COL_ZSSD = 0, 1536, 2048, 2560
COL_XLRU, COL_GLRU, COL_GMERGE, COL_BC, COL_US5 = 3072, 3584, 4096, 8192, 8448
MAIN_COLS = 8832
SMALL_COLS = 128
SMALL_B, SMALL_A, SMALL_DT = 0, 4, 8

SUBLANES = 8
CHUNK = 128
LRU_TILE = 256
VMEM_LIMIT = 56 * 1024 * 1024


def _sigmoid(x):
    return 1.0 / (1.0 + jnp.exp(-x))


def _silu(x):
    return x * _sigmoid(x)


def _softplus(x):
    return jnp.maximum(x, 0.0) + jnp.log1p(jnp.exp(-jnp.abs(x)))


def _gelu(x):
    return jax.nn.gelu(x, approximate=True)


def _dot(a, b):
    return jnp.dot(a, b, preferred_element_type=F32)


def _dot_nt(a, b):
    return lax.dot_general(a, b, (((1,), (1,)), ((), ())), preferred_element_type=F32)


def _dot_tn(a, b):
    return lax.dot_general(a, b, (((0,), (0,)), ((), ())), preferred_element_type=F32)


def _dot_exact(a, b):
    return jnp.dot(a, b, preferred_element_type=F32, precision=lax.Precision.HIGHEST)


def _params(*sem):
    return pltpu.CompilerParams(dimension_semantics=sem, vmem_limit_bytes=VMEM_LIMIT)


def _full(shape):
    n = len(shape)
    return pl.BlockSpec(shape, lambda *_: (0,) * n)


def _ada_kernel(c_ref, w_ref, b_ref, o_ref):
    cond = _silu(c_ref[...]).astype(BF16)
    o_ref[...] = _dot(cond, w_ref[...].astype(BF16)) + b_ref[...]


def _ada_mod(c, ada_w, ada_b):
    depth, d, n = ada_w.shape
    bsz = c.shape[0]
    tn = 1536
    return pl.pallas_call(
        _ada_kernel,
        grid=(depth, n // tn),
        in_specs=[
            pl.BlockSpec((bsz, d), lambda l, j: (0, 0)),
            pl.BlockSpec((None, d, tn), lambda l, j: (l, 0, j)),
            pl.BlockSpec((None, 1, tn), lambda l, j: (l, 0, j)),
        ],
        out_specs=pl.BlockSpec((None, bsz, tn), lambda l, j: (l, 0, j)),
        out_shape=jax.ShapeDtypeStruct((depth, bsz, n), F32),
        compiler_params=_params("parallel", "parallel"),
        name="ada_mod",
    )(c, ada_w, ada_b.reshape(depth, 1, n))


def _modulated_norm(x, g, shift, scale):
    ms = jnp.mean(x * x, axis=-1, keepdims=True)
    return x * lax.rsqrt(ms + NORM_EPS) * g * (1.0 + scale) + shift


def _inproj_kernel(x_ref, mod_ref, g_ref, w_ref, ws_ref, p_ref, small_ref, h_ref):
    @pl.when(pl.program_id(1) == 0)
    def _():
        h = _modulated_norm(x_ref[...], g_ref[...], mod_ref[:, 0:D_MODEL], mod_ref[:, D_MODEL:2 * D_MODEL])
        hb = h.astype(BF16)
        h_ref[...] = hb
        small_ref[...] = _dot(hb, ws_ref[...])

    p_ref[...] = _dot(h_ref[...], w_ref[...]).astype(BF16)


def _inproj(x2, mod3, ln_g, w_main, w_small, seq):
    t = x2.shape[0]
    tm = min(1024, seq)
    nj = 3
    tn = MAIN_COLS // nj
    per_b = seq // tm
    return pl.pallas_call(
        _inproj_kernel,
        grid=(t // tm, nj),
        in_specs=[
            pl.BlockSpec((tm, D_MODEL), lambda i, j: (i, 0)),
            pl.BlockSpec((None, 1, 6 * D_MODEL), lambda i, j: (i // per_b, 0, 0)),
            pl.BlockSpec((1, D_MODEL), lambda i, j: (0, 0)),
            pl.BlockSpec((D_MODEL, tn), lambda i, j: (0, j)),
            pl.BlockSpec((D_MODEL, SMALL_COLS), lambda i, j: (0, 0)),
        ],
        out_specs=[
            pl.BlockSpec((tm, tn), lambda i, j: (i, j)),
            pl.BlockSpec((tm, SMALL_COLS), lambda i, j: (i, 0)),
        ],
        out_shape=[
            jax.ShapeDtypeStruct((t, MAIN_COLS), BF16),
            jax.ShapeDtypeStruct((t, SMALL_COLS), F32),
        ],
        scratch_shapes=[pltpu.VMEM((tm, D_MODEL), BF16)],
        compiler_params=_params("parallel", "arbitrary"),
        name="inproj",
    )(x2, mod3, ln_g, w_main, w_small)


def _causal_conv(x, xpad_ref, w_ref, b_ref, first):
    ts = x.shape[0]

    @pl.when(first)
    def _():
        xpad_ref[0:SUBLANES, :] = jnp.zeros((SUBLANES, x.shape[1]), F32)

    xpad_ref[SUBLANES:SUBLANES + ts, :] = x
    y = b_ref[...] + w_ref[CONV_K - 1:CONV_K, :] * x
    for k in range(CONV_K - 1):
        start = SUBLANES - (CONV_K - 1) + k
        y = y + w_ref[k:k + 1, :] * xpad_ref[start:start + ts, :]
    xpad_ref[0:SUBLANES, :] = x[ts - SUBLANES:, :]
    return y


def _s5_kernel(u_ref, wb_ref, pneg_ref, ppos_ref, lam1_ref, tri_ref, wc_ref, d_ref, gw_ref, gb_ref,
               o_ref, carry_ref):
    h = S5_HALF

    @pl.when(pl.program_id(1) == 0)
    def _():
        carry_ref[...] = jnp.zeros_like(carry_ref)

    ub = u_ref[...]
    bu = _dot(ub, wb_ref[...])
    br, bi = bu[:, :h], bu[:, h:]
    nr, ni = pneg_ref[:, :h], pneg_ref[:, h:]
    z = jnp.concatenate([nr * br - ni * bi, nr * bi + ni * br], axis=-1).astype(BF16)
    s = _dot(tri_ref[...], z)
    cr, ci = carry_ref[:, :h], carry_ref[:, h:]
    l1r, l1i = lam1_ref[:, :h], lam1_ref[:, h:]
    sr = s[:, :h] + (l1r * cr - l1i * ci)
    si = s[:, h:] + (l1r * ci + l1i * cr)
    pr, pi = ppos_ref[:, :h], ppos_ref[:, h:]
    xr = pr * sr - pi * si
    xi = pr * si + pi * sr
    ts = xr.shape[0]
    carry_ref[:, :h] = xr[ts - 1:ts, :]
    carry_ref[:, h:] = xi[ts - 1:ts, :]
    xcat = jnp.concatenate([xr, xi], axis=-1).astype(BF16)
    y = _dot(xcat, wc_ref[...]) + d_ref[...] * ub.astype(F32)
    y = _gelu(y)
    o_ref[...] = (y * _sigmoid(_dot(y.astype(BF16), gw_ref[...]) + gb_ref[...])).astype(BF16)


def _s5_tables(lam_re, lam_im, log_dt, b_re, b_im, c_re, c_im, ts):
    dt = jnp.exp(log_dt)[:, None]
    ar, ai = lam_re * dt, lam_im * dt
    er = jnp.exp(ar)
    lbr, lbi = er * jnp.cos(ai), er * jnp.sin(ai)
    den = lam_re * lam_re + lam_im * lam_im
    qr = ((lbr - 1.0) * lam_re + lbi * lam_im) / den
    qi = (lbi * lam_re - (lbr - 1.0) * lam_im) / den
    bbr = qr[..., None] * b_re - qi[..., None] * b_im
    bbi = qr[..., None] * b_im + qi[..., None] * b_re
    eye = jnp.eye(S5_GROUPS, dtype=F32)

    def blockdiag_in(m):
        return jnp.einsum('gph,gk->ghkp', m, eye).reshape(S5_WIDTH, S5_HALF)

    def blockdiag_out(m):
        return jnp.einsum('ghp,gk->gpkh', m, eye).reshape(S5_HALF, S5_WIDTH)

    wb = jnp.concatenate([blockdiag_in(bbr), blockdiag_in(bbi)], axis=1).astype(BF16)
    wc = jnp.concatenate([blockdiag_out(c_re), -blockdiag_out(c_im)], axis=0).astype(BF16)
    j = jnp.arange(ts, dtype=F32)[:, None]
    arf, aif = ar.reshape(1, S5_HALF), ai.reshape(1, S5_HALF)

    def power(sign):
        mag = jnp.exp(sign * j * arf)
        ang = sign * j * aif
        return jnp.concatenate([mag * jnp.cos(ang), mag * jnp.sin(ang)], axis=1)

    lam1 = jnp.concatenate([lbr.reshape(1, S5_HALF), lbi.reshape(1, S5_HALF)], axis=1)
    return wb, power(-1.0), power(1.0), lam1, wc


def _s5_mixer(p_main, bsz, seq, tables, d, glu_w, glu_b):
    ts = min(CHUNK, seq)
    wb, pneg, ppos, lam1, wc = tables
    tri = jnp.tril(jnp.ones((ts, ts), F32)).astype(BF16)
    nsb = seq // ts
    ucol = COL_US5 // S5_WIDTH
    consts = [wb, pneg, ppos, lam1, tri, wc, d.reshape(1, -1), glu_w.astype(BF16), glu_b.reshape(1, -1)]
    return pl.pallas_call(
        _s5_kernel,
        grid=(bsz, nsb),
        in_specs=[pl.BlockSpec((ts, S5_WIDTH), lambda b, s: (b * nsb + s, ucol))]
        + [_full(a.shape) for a in consts],
        out_specs=pl.BlockSpec((ts, S5_WIDTH), lambda b, s: (b * nsb + s, 0)),
        out_shape=jax.ShapeDtypeStruct((bsz * seq, S5_WIDTH), BF16),
        scratch_shapes=[pltpu.VMEM((1, 2 * S5_HALF), F32)],
        compiler_params=_params("parallel", "arbitrary"),
        name="s5_mixer",
    )(p_main, *consts)


def _dot3(a, b):
    ah = a.astype(BF16)
    al = (a - ah.astype(F32)).astype(BF16)
    bh = b.astype(BF16)
    bl = (b - bh.astype(F32)).astype(BF16)
    return _dot(ah, bh) + (_dot(ah, bl) + _dot(al, bh))


INV_BASE = 16


def _unit_lower_inverse(a, ri, ci):
    n = a.shape[0]
    eye = (ri == ci).astype(F32)
    pw = -jnp.where(ri // INV_BASE == ci // INV_BASE, a, 0.0)
    inv = eye + pw
    for _ in range(INV_BASE.bit_length() - 2):
        pw = _dot3(pw, pw)
        inv = inv + _dot3(inv, pw)
    size = INV_BASE
    while size < n:
        lower_left = (ri // (2 * size) == ci // (2 * size)) & ((ri // size) % 2 == 1) & ((ci // size) % 2 == 0)
        inv = inv - _dot3(_dot3(inv, jnp.where(lower_left, a, 0.0)), inv)
        size *= 2
    return inv


def _gdn_kernel(qkv_ref, z_ref, small_ref, cw_ref, cb_ref, prm_ref, nw_ref, tri_ref, o_ref, xpad_ref, state_ref):
    first = pl.program_id(1) == 0

    @pl.when(first)
    def _():
        state_ref[...] = jnp.zeros_like(state_ref)

    xc = _silu(_causal_conv(qkv_ref[...].astype(F32), xpad_ref, cw_ref, cb_ref, first))
    ts = xc.shape[0]
    sm = small_ref[...]
    beta_all = _sigmoid(sm)
    g_all = -jnp.exp(prm_ref[0:1, :]) * _softplus(sm + prm_ref[1:2, :])
    gc = _dot_exact(tri_ref[...], g_all)
    gct = gc.T
    ri = lax.broadcasted_iota(jnp.int32, (ts, ts), 0)
    ci = lax.broadcasted_iota(jnp.int32, (ts, ts), 1)
    for hd in range(GDN_HEADS):
        q = xc[:, hd * GDN_DK:(hd + 1) * GDN_DK]
        k = xc[:, GDN_QK + hd * GDN_DK:GDN_QK + (hd + 1) * GDN_DK]
        v = xc[:, 2 * GDN_QK + hd * GDN_DV:2 * GDN_QK + (hd + 1) * GDN_DV]
        q = q * lax.rsqrt(jnp.sum(q * q, axis=-1, keepdims=True) + NORM_EPS) * (GDN_DK ** -0.5)
        k = k * lax.rsqrt(jnp.sum(k * k, axis=-1, keepdims=True) + NORM_EPS)
        col = gc[:, SMALL_A + hd:SMALL_A + hd + 1]
        row = gct[SMALL_A + hd:SMALL_A + hd + 1, :]
        decay = jnp.exp(jnp.where(ri >= ci, col - row, -jnp.inf))
        beta = beta_all[:, SMALL_B + hd:SMALL_B + hd + 1]
        kb = k * beta
        kbf = k.astype(BF16)
        a = jnp.where(ri > ci, _dot_nt(kb.astype(BF16), kbf) * decay, 0.0)
        inv = _unit_lower_inverse(a, ri, ci)
        eg = jnp.exp(col)
        rhs = jnp.concatenate([kb * eg, v * beta], axis=-1).astype(BF16)
        sol = _dot(inv.astype(BF16), rhs)
        w, u = sol[:, :GDN_DK], sol[:, GDN_DK:]
        attn = _dot_nt(q.astype(BF16), kbf) * decay
        g_last = gc[ts - 1:ts, SMALL_A + hd:SMALL_A + hd + 1]
        q_dec = q * eg
        k_dec = k * jnp.exp(g_last - col)
        st = state_ref[hd]
        stb = st.astype(BF16)
        v_new = u - _dot(w.astype(BF16), stb)
        vnb = v_new.astype(BF16)
        o = _dot(q_dec.astype(BF16), stb) + _dot(attn.astype(BF16), vnb)
        state_ref[hd] = st * jnp.exp(g_last) + _dot_tn(k_dec.astype(BF16), vnb)
        o = o * lax.rsqrt(jnp.mean(o * o, axis=-1, keepdims=True) + NORM_EPS)
        zh = z_ref[:, hd * GDN_DV:(hd + 1) * GDN_DV].astype(F32)
        o_ref[:, hd * GDN_DV:(hd + 1) * GDN_DV] = (o * nw_ref[...] * _silu(zh)).astype(BF16)


def _pad_lanes(v, start):
    return jnp.zeros((SMALL_COLS,), F32).at[start:start + v.shape[0]].set(v)


def _gdn_mixer(p_main, small, bsz, seq, conv_w, conv_b, a_log, dt_bias, norm_w):
    ts = min(CHUNK, seq)
    nsb = seq // ts
    width = 3 * GDN_QK
    prm = jnp.stack([_pad_lanes(a_log, SMALL_A), _pad_lanes(dt_bias, SMALL_A)])
    tri = jnp.tril(jnp.ones((ts, ts), F32))
    consts = [conv_w, conv_b.reshape(1, -1), prm, norm_w.reshape(1, -1), tri]
    return pl.pallas_call(
        _gdn_kernel,
        grid=(bsz, nsb),
        in_specs=[
            pl.BlockSpec((ts, width), lambda b, s: (b * nsb + s, COL_QKV // width)),
            pl.BlockSpec((ts, GDN_WIDTH), lambda b, s: (b * nsb + s, COL_ZGDN // GDN_WIDTH)),
            pl.BlockSpec((ts, SMALL_COLS), lambda b, s: (b * nsb + s, 0)),
        ] + [_full(a.shape) for a in consts],
        out_specs=pl.BlockSpec((ts, GDN_WIDTH), lambda b, s: (b * nsb + s, 0)),
        out_shape=jax.ShapeDtypeStruct((bsz * seq, GDN_WIDTH), BF16),
        scratch_shapes=[
            pltpu.VMEM((ts + SUBLANES, width), F32),
            pltpu.VMEM((GDN_HEADS, GDN_DK, GDN_DV), F32),
        ],
        compiler_params=_params("parallel", "arbitrary"),
        name="gdn_mixer",
    )(p_main, p_main, small, *consts)


def _ssd_kernel(xz_ref, bc_ref, small_ref, cwx_ref, cbx_ref, cwbc_ref, cbbc_ref, prm_ref, hv_ref, tri_ref,
                expand_ref, smask_ref, o_ref, xpad_ref, bcpad_ref, state_ref):
    first = pl.program_id(1) == 0

    @pl.when(first)
    def _():
        state_ref[...] = jnp.zeros_like(state_ref)

    x = _silu(_causal_conv(xz_ref[:, :SSD_WIDTH].astype(F32), xpad_ref, cwx_ref, cbx_ref, first))
    bc = _silu(_causal_conv(bc_ref[...].astype(F32), bcpad_ref, cwbc_ref, cbbc_ref, first))
    ts = x.shape[0]
    bm, cm = bc[:, :SSD_BC], bc[:, SSD_BC:]
    sm = small_ref[...]
    dt_all = _softplus(sm + prm_ref[1:2, :])
    la_all = dt_all * (-jnp.exp(prm_ref[0:1, :]))
    cs = _dot_exact(tri_ref[...], la_all)
    cst = cs.T
    expand = expand_ref[...]
    dtx = _dot_exact(dt_all, expand)
    csx = _dot_exact(cs, expand)
    cs_last_x = csx[ts - 1:ts, :]
    xdt = x * dtx
    xdtb = xdt.astype(BF16)
    ri = lax.broadcasted_iota(jnp.int32, (ts, ts), 0)
    ci = lax.broadcasted_iota(jnp.int32, (ts, ts), 1)
    lane = lax.broadcasted_iota(jnp.int32, (ts, SSD_BC), 1)
    bmb = bm.astype(BF16)
    cb = []
    for g in range(SSD_GROUPS):
        in_g = (lane >= g * SSD_STATE) & (lane < (g + 1) * SSD_STATE)
        cb.append(_dot_nt(jnp.where(in_g, cm, 0.0).astype(BF16), bmb))
    halves = []
    for pair in range(SSD_HEADS // 2):
        rhs = xdtb[:, pair * 128:(pair + 1) * 128]
        ys = []
        for hd in (2 * pair, 2 * pair + 1):
            col = cs[:, SMALL_DT + hd:SMALL_DT + hd + 1]
            row = cst[SMALL_DT + hd:SMALL_DT + hd + 1, :]
            decay = jnp.exp(jnp.where(ri >= ci, col - row, -jnp.inf))
            ys.append(_dot((cb[hd // SSD_HPG] * decay).astype(BF16), rhs))
        halves.append(jnp.where(lane < SSD_HEAD_DIM, ys[0], ys[1]))
    y = jnp.concatenate(halves, axis=-1)
    st = state_ref[...]
    y = y + _dot(cm.astype(BF16), st.astype(BF16)) * jnp.exp(csx)
    to_end = jnp.exp(cs_last_x - csx)
    upd = _dot_tn(bmb, (xdt * to_end).astype(BF16))
    state_ref[...] = st * jnp.exp(cs_last_x) + upd * smask_ref[...]
    y = y + hv_ref[0:1, :] * x
    y = y * _silu(xz_ref[:, SSD_WIDTH:].astype(F32))
    y = y * lax.rsqrt(jnp.mean(y * y, axis=-1, keepdims=True) + NORM_EPS)
    o_ref[...] = (y * hv_ref[1:2, :]).astype(BF16)


def _ssd_mixer(p_main, small, bsz, seq, cw_x, cb_x, cw_bc, cb_bc, a_log, dt_bias, d, norm_w):
    ts = min(CHUNK, seq)
    nsb = seq // ts
    prm = jnp.stack([_pad_lanes(a_log, SMALL_DT), _pad_lanes(dt_bias, SMALL_DT)])
    hv = jnp.stack([jnp.repeat(d, SSD_HEAD_DIM), norm_w])
    tri = jnp.tril(jnp.ones((ts, ts), F32))
    lane_head = jnp.arange(SSD_WIDTH) // SSD_HEAD_DIM
    expand = (jnp.arange(SMALL_COLS)[:, None] == SMALL_DT + lane_head[None, :]).astype(F32)
    row_group = jnp.arange(SSD_BC) // SSD_STATE
    smask = (row_group[:, None] == (lane_head // SSD_HPG)[None, :]).astype(F32)
    consts = [cw_x, cb_x.reshape(1, -1), cw_bc, cb_bc.reshape(1, -1), prm, hv, tri, expand, smask]
    xz_w = 2 * SSD_WIDTH
    bc_w = 2 * SSD_BC
    return pl.pallas_call(
        _ssd_kernel,
        grid=(bsz, nsb),
        in_specs=[
            pl.BlockSpec((ts, xz_w), lambda b, s: (b * nsb + s, COL_XSSD // xz_w)),
            pl.BlockSpec((ts, bc_w), lambda b, s: (b * nsb + s, COL_BC // bc_w)),
            pl.BlockSpec((ts, SMALL_COLS), lambda b, s: (b * nsb + s, 0)),
        ] + [_full(a.shape) for a in consts],
        out_specs=pl.BlockSpec((ts, SSD_WIDTH), lambda b, s: (b * nsb + s, 0)),
        out_shape=jax.ShapeDtypeStruct((bsz * seq, SSD_WIDTH), BF16),
        scratch_shapes=[
            pltpu.VMEM((ts + SUBLANES, SSD_WIDTH), F32),
            pltpu.VMEM((ts + SUBLANES, bc_w), F32),
            pltpu.VMEM((SSD_BC, SSD_WIDTH), F32),
        ],
        compiler_params=_params("parallel", "arbitrary"),
        name="ssd_mixer",
    )(p_main, p_main, small, *consts)


def _lru_kernel(xg_ref, cw_ref, cb_ref, wr_ref, wi_ref, vec_ref, o_ref, xpad_ref, h_ref):
    step = pl.program_id(1)
    first = step == 0

    @pl.when(first)
    def _():
        h_ref[...] = jnp.zeros_like(h_ref)

    x = _causal_conv(xg_ref[:, :LRU_WIDTH].astype(F32), xpad_ref, cw_ref, cb_ref, first)
    ts = x.shape[0]
    xb = x.astype(BF16)
    r = _sigmoid(_dot(xb, wr_ref[...]) + vec_ref[0:1, :])
    gate_i = _sigmoid(_dot(xb, wi_ref[...]) + vec_ref[1:2, :])
    log_a = -LRU_C * r * _softplus(-vec_ref[2:3, :])
    a = jnp.exp(log_a)
    mult = jnp.sqrt(-jnp.tanh(log_a) * (a * a + 1.0))
    row = lax.broadcasted_iota(jnp.int32, (ts, LRU_WIDTH), 0)
    mult = jnp.where(row + step * ts == 0, 1.0, mult)
    b = mult * gate_i * x
    shift = 1
    while shift < ts:
        keep = row >= shift
        a_prev = jnp.where(keep, pltpu.roll(a, shift, 0), 1.0)
        b_prev = jnp.where(keep, pltpu.roll(b, shift, 0), 0.0)
        b = a * b_prev + b
        a = a * a_prev
        shift *= 2
    h = b + a * h_ref[...]
    h_ref[...] = h[ts - 1:ts, :]
    o_ref[...] = (h * _gelu(xg_ref[:, LRU_WIDTH:].astype(F32))).astype(BF16)


def _lru_mixer(p_main, bsz, seq, conv_w, conv_b, lam, wr, br, wi, bi):
    ts = min(LRU_TILE, seq)
    nsb = seq // ts
    eye = jnp.eye(LRU_BLOCKS, dtype=F32)

    def blockdiag(w):
        return jnp.einsum('nde,nm->ndme', w, eye).reshape(LRU_WIDTH, LRU_WIDTH).astype(BF16)

    vec = jnp.stack([br, bi, lam])
    consts = [conv_w, conv_b.reshape(1, -1), blockdiag(wr), blockdiag(wi), vec]
    width = 2 * LRU_WIDTH
    return pl.pallas_call(
        _lru_kernel,
        grid=(bsz, nsb),
        in_specs=[pl.BlockSpec((ts, width), lambda b, s: (b * nsb + s, COL_XLRU // width))]
        + [_full(a.shape) for a in consts],
        out_specs=pl.BlockSpec((ts, LRU_WIDTH), lambda b, s: (b * nsb + s, 0)),
        out_shape=jax.ShapeDtypeStruct((bsz * seq, LRU_WIDTH), BF16),
        scratch_shapes=[
            pltpu.VMEM((ts + SUBLANES, LRU_WIDTH), F32),
            pltpu.VMEM((1, LRU_WIDTH), F32),
        ],
        compiler_params=_params("parallel", "arbitrary"),
        name="lru_mixer",
    )(p_main, *consts)


def _merge_kernel(ya_ref, yb_ref, yc_ref, yd_ref, gm_ref, x_ref, mod_ref, wa_ref, wb_ref, wc_ref, wd_ref,
                  wo_ref, o_ref):
    merged = None
    for i, (y_ref, w_ref) in enumerate(((ya_ref, wa_ref), (yb_ref, wb_ref), (yc_ref, wc_ref), (yd_ref, wd_ref))):
        gate = _sigmoid(gm_ref[:, i * D_MODEL:(i + 1) * D_MODEL].astype(F32))
        term = gate * _dot(y_ref[...], w_ref[...])
        merged = term if merged is None else merged + term
    out = _dot(merged.astype(BF16), wo_ref[...])
    o_ref[...] = x_ref[...] + mod_ref[:, 2 * D_MODEL:3 * D_MODEL] * out


def _merge(ys, p_main, x2, mod3, w_branch, w_out, seq):
    t = x2.shape[0]
    tm = min(512, seq)
    per_b = seq // tm
    widths = (S5_WIDTH, GDN_WIDTH, SSD_WIDTH, LRU_WIDTH)
    rows, acc = [], 0
    for wd in widths:
        rows.append(w_branch[acc:acc + wd].astype(BF16))
        acc += wd
    gw = N_BRANCH * D_MODEL
    return pl.pallas_call(
        _merge_kernel,
        grid=(t // tm,),
        in_specs=[pl.BlockSpec((tm, wd), lambda i: (i, 0)) for wd in widths] + [
            pl.BlockSpec((tm, gw), lambda i: (i, COL_GMERGE // gw)),
            pl.BlockSpec((tm, D_MODEL), lambda i: (i, 0)),
            pl.BlockSpec((None, 1, 6 * D_MODEL), lambda i: (i // per_b, 0, 0)),
        ] + [_full(r.shape) for r in rows] + [_full((D_MODEL, D_MODEL))],
        out_specs=pl.BlockSpec((tm, D_MODEL), lambda i: (i, 0)),
        out_shape=jax.ShapeDtypeStruct((t, D_MODEL), F32),
        compiler_params=_params("parallel"),
        name="merge_out",
    )(*ys, p_main, x2, mod3, *rows, w_out.astype(BF16))


FFN_SLICES = tuple((s, min(512, FFN_HIDDEN - s)) for s in range(0, FFN_HIDDEN, 512))


def _ffn_kernel(x_ref, mod_ref, g_ref, w13_ref, w2_ref, gf_ref, o_ref, *, final):
    x = x_ref[...]
    h = _modulated_norm(x, g_ref[...], mod_ref[:, 3 * D_MODEL:4 * D_MODEL], mod_ref[:, 4 * D_MODEL:5 * D_MODEL])
    hb = h.astype(BF16)
    acc = None
    for start, size in FFN_SLICES:
        a = _dot(hb, w13_ref[:, start:start + size])
        b = _dot(hb, w13_ref[:, FFN_HIDDEN + start:FFN_HIDDEN + start + size])
        part = _dot((_silu(a) * b).astype(BF16), w2_ref[start:start + size, :])
        acc = part if acc is None else acc + part
    x = x + mod_ref[:, 5 * D_MODEL:6 * D_MODEL] * acc
    if final:
        x = x * lax.rsqrt(jnp.mean(x * x, axis=-1, keepdims=True) + NORM_EPS) * gf_ref[...]
    o_ref[...] = x


def _ffn(x2, mod3, ln_g, w13, w2, ln_final, seq, final):
    t = x2.shape[0]
    tm = min(512, seq)
    per_b = seq // tm
    return pl.pallas_call(
        functools.partial(_ffn_kernel, final=final),
        grid=(t // tm,),
        in_specs=[
            pl.BlockSpec((tm, D_MODEL), lambda i: (i, 0)),
            pl.BlockSpec((None, 1, 6 * D_MODEL), lambda i: (i // per_b, 0, 0)),
            _full((1, D_MODEL)),
            pl.BlockSpec((D_MODEL, 2 * FFN_HIDDEN), lambda i: (0, 0), pipeline_mode=pl.Buffered(1)),
            pl.BlockSpec((FFN_HIDDEN, D_MODEL), lambda i: (0, 0), pipeline_mode=pl.Buffered(1)),
            _full((1, D_MODEL)),
        ],
        out_specs=pl.BlockSpec((tm, D_MODEL), lambda i: (i, 0)),
        out_shape=jax.ShapeDtypeStruct((t, D_MODEL), F32),
        compiler_params=_params("parallel"),
        name="ffn_final" if final else "ffn",
    )(x2, mod3, ln_g, w13, w2, ln_final)


def _split_in_weights(w_in):
    o = 0
    seg = {}
    for name, width in (("q", 512), ("k", 512), ("v", 512), ("xssd", 512), ("bssd", 128), ("cssd", 128),
                        ("xlru", 512), ("us5", 384), ("bgdn", 4), ("agdn", 4), ("zgdn", 512), ("zssd", 512),
                        ("dtssd", 8), ("glru", 512), ("gmerge", 4096)):
        seg[name] = w_in[:, o:o + width]
        o += width
    main = jnp.concatenate([seg[n] for n in ("q", "k", "v", "zgdn", "xssd", "zssd", "xlru", "glru", "gmerge",
                                             "bssd", "cssd", "us5")], axis=1).astype(BF16)
    pad = jnp.zeros((w_in.shape[0], SMALL_COLS - 16), w_in.dtype)
    small = jnp.concatenate([seg["bgdn"], seg["agdn"], seg["dtssd"], pad], axis=1).astype(BF16)
    return main, small


def kernel(x, c, ln_mix_g, ln_ffn_g, ln_final_g, ada_w, ada_b, w_in, conv_w, conv_b, s5_lambda_re, s5_lambda_im, s5_log_dt, s5_b_re, s5_b_im, s5_c_re, s5_c_im, s5_d, s5_glu_w, s5_glu_b, gdn_a_log, gdn_dt_bias, gdn_norm_w, ssd_a_log, ssd_dt_bias, ssd_d, ssd_norm_w, lru_lambda, lru_wr, lru_br, lru_wi, lru_bi, w_branch, w_out, ffn_w13, ffn_w2):
    bsz, seq, d = x.shape
    t = bsz * seq
    x2 = x.reshape(t, d)
    mod = _ada_mod(c, ada_w, ada_b)
    ln_final = ln_final_g.reshape(1, d)
    for l in range(DEPTH):
        mod3 = mod[l].reshape(bsz, 1, 6 * d)
        w_main, w_small = _split_in_weights(w_in[l])
        p_main, small = _inproj(x2, mod3, ln_mix_g[l].reshape(1, d), w_main, w_small, seq)

        cw, cb = conv_w[l], conv_b[l]
        tables = _s5_tables(s5_lambda_re[l], s5_lambda_im[l], s5_log_dt[l], s5_b_re[l], s5_b_im[l],
                            s5_c_re[l], s5_c_im[l], min(CHUNK, seq))
        y_a = _s5_mixer(p_main, bsz, seq, tables, s5_d[l], s5_glu_w[l], s5_glu_b[l])
        y_b = _gdn_mixer(p_main, small, bsz, seq, cw[:, 0:1536], cb[0:1536], gdn_a_log[l], gdn_dt_bias[l],
                         gdn_norm_w[l])
        y_c = _ssd_mixer(p_main, small, bsz, seq, cw[:, 1536:2048], cb[1536:2048], cw[:, 2048:2304],
                         cb[2048:2304], ssd_a_log[l], ssd_dt_bias[l], ssd_d[l], ssd_norm_w[l])
        y_d = _lru_mixer(p_main, bsz, seq, cw[:, 2304:2816], cb[2304:2816], lru_lambda[l], lru_wr[l],
                         lru_br[l], lru_wi[l], lru_bi[l])
        x2 = _merge((y_a, y_b, y_c, y_d), p_main, x2, mod3, w_branch[l], w_out[l], seq)
        x2 = _ffn(x2, mod3, ln_ffn_g[l].reshape(1, d), ffn_w13[l].astype(BF16), ffn_w2[l].astype(BF16),
                  ln_final, seq, final=(l == DEPTH - 1))
    return x2.reshape(bsz, seq, d)
```

```python
import functools

import jax
import jax.numpy as jnp
from jax import lax
from jax.experimental import pallas as pl
from jax.experimental.pallas import tpu as pltpu

F32 = jnp.float32
BF16 = jnp.bfloat16

D_MODEL = 1024
DEPTH = 2
CONV_K = 4
NORM_EPS = 1e-6
S5_GROUP, S5_STATE, S5_WIDTH, S5_GROUPS = 16, 64, 384, 24
GDN_HEADS, GDN_DK, GDN_DV = 4, 128, 128
GDN_QK = GDN_HEADS * GDN_DK
GDN_WIDTH = GDN_HEADS * GDN_DV
SSD_HEAD_DIM, SSD_WIDTH, SSD_HEADS, SSD_GROUPS, SSD_HPG, SSD_STATE = 64, 512, 8, 2, 4, 64
SSD_BC = SSD_GROUPS * SSD_STATE
LRU_WIDTH, LRU_BLOCK, LRU_BLOCKS, LRU_C = 512, 64, 8, 8.0
N_BRANCH = 4
FFN_HIDDEN = 2816

LANES = 128
SUBLANES = 8

COL_QKV, COL_XSSD, COL_XLRU, COL_BC = 0, 1536, 2048, 2560
CONV_COLS = 2816
SILU_CONV_END = 2048
COL_US5A, COL_ZGDN, COL_ZSSD, COL_GMERGE, COL_GLRU, COL_US5B = 2816, 3072, 3584, 4096, 8192, 8704
MAIN_COLS = 8832
PROJ_TILES = 3
PROJ_TN = MAIN_COLS // PROJ_TILES
PROJ_ROW_CHUNKS = 2
SMALL_COLS = 128
SMALL_B, SMALL_A, SMALL_DT = 0, 4, 8

CHUNK = 128
GDN_TILE = 256
LRU_TILE = 256
S5_PARTS = S5_WIDTH // LANES
S5_PART_STATE = (S5_GROUPS // S5_PARTS) * S5_STATE
VMEM_LIMIT = 56 * 1024 * 1024


def _sigmoid(x):
    return 0.5 * jnp.tanh(0.5 * x) + 0.5


def _silu(x):
    return x * _sigmoid(x)


def _softplus(x):
    return jnp.maximum(x, 0.0) + jnp.log1p(jnp.exp(-jnp.abs(x)))


def _gelu(x):
    return jax.nn.gelu(x, approximate=True)


def _dot(a, b):
    return jnp.dot(a, b, preferred_element_type=F32)


def _dot_nt(a, b):
    return lax.dot_general(a, b, (((1,), (1,)), ((), ())), preferred_element_type=F32)


def _dot_tn(a, b):
    return lax.dot_general(a, b, (((0,), (0,)), ((), ())), preferred_element_type=F32)


def _bmm(a, b):
    return jnp.einsum('gij,gjk->gik', a, b, preferred_element_type=F32)


def _bmm_nt(a, b):
    return jnp.einsum('gik,gjk->gij', a, b, preferred_element_type=F32)


def _split3(x):
    x1 = x.astype(BF16)
    r1 = x - x1.astype(F32)
    x2 = r1.astype(BF16)
    x3 = (r1 - x2.astype(F32)).astype(BF16)
    return x1, x2, x3


def _select_rows(sel_bf16, x):
    x1, x2, x3 = _split3(x)
    return _dot(sel_bf16, x1) + (_dot(sel_bf16, x2) + _dot(sel_bf16, x3))


def _select_cols(x, sel_bf16):
    x1, x2, x3 = _split3(x)
    return _dot(x1, sel_bf16) + (_dot(x2, sel_bf16) + _dot(x3, sel_bf16))


def _lane_sum(x):
    shape = x.shape
    ones = jnp.ones((shape[-1], shape[-1]), BF16)
    return _dot(x.reshape(-1, shape[-1]).astype(BF16), ones).reshape(shape)


def _params(*sem):
    return pltpu.CompilerParams(dimension_semantics=sem, vmem_limit_bytes=VMEM_LIMIT)


def _full(shape):
    n = len(shape)
    return pl.BlockSpec(shape, lambda *_: (0,) * n)


def _ada_kernel(c_ref, w_ref, b_ref, o_ref):
    cond = _silu(c_ref[...]).astype(BF16)
    o_ref[...] = _dot(cond, w_ref[...].astype(BF16)) + b_ref[...]


def _ada_mod(c, ada_w, ada_b):
    depth, d, n = ada_w.shape
    bsz = c.shape[0]
    tn = 1536
    return pl.pallas_call(
        _ada_kernel,
        grid=(depth, n // tn),
        in_specs=[
            pl.BlockSpec((bsz, d), lambda l, j: (0, 0)),
            pl.BlockSpec((None, d, tn), lambda l, j: (l, 0, j)),
            pl.BlockSpec((None, 1, tn), lambda l, j: (l, 0, j)),
        ],
        out_specs=pl.BlockSpec((None, bsz, tn), lambda l, j: (l, 0, j)),
        out_shape=jax.ShapeDtypeStruct((depth, bsz, n), F32),
        compiler_params=_params("parallel", "parallel"),
        name="ada_mod",
    )(c, ada_w, ada_b.reshape(depth, 1, n))


def _modulated_norm(x, g, shift, scale):
    ms = jnp.mean(x * x, axis=-1, keepdims=True)
    return x * lax.rsqrt(ms + NORM_EPS) * g * (1.0 + scale) + shift


def _causal_conv(x, xpad_ref, w_ref, b_ref, first):
    ts = x.shape[0]

    if first is not None:
        @pl.when(first)
        def _():
            xpad_ref[0:SUBLANES, :] = jnp.zeros((SUBLANES, x.shape[1]), F32)

    xpad_ref[SUBLANES:SUBLANES + ts, :] = x
    y = b_ref[...] + w_ref[CONV_K - 1:CONV_K, :] * x
    for k in range(CONV_K - 1):
        start = SUBLANES - (CONV_K - 1) + k
        y = y + w_ref[k:k + 1, :] * xpad_ref[start:start + ts, :]
    xpad_ref[0:SUBLANES, :] = x[ts - SUBLANES:, :]
    return y


def _inproj_kernel(x_ref, mod_ref, g_ref, w_ref, ws_ref, cw_ref, cb_ref, p_ref, small_ref, xpad_ref, *, per_b):
    j = pl.program_id(0)
    i = pl.program_id(1)
    h = _modulated_norm(x_ref[...], g_ref[...], mod_ref[:, 0:D_MODEL], mod_ref[:, D_MODEL:2 * D_MODEL])
    hb = h.astype(BF16)
    small_ref[...] = _dot(hb, ws_ref[...])
    rows = hb.shape[0] // PROJ_ROW_CHUNKS

    def chunked(epilogue):
        for rc in range(PROJ_ROW_CHUNKS):
            r0 = rc * rows
            r = _dot(hb[r0:r0 + rows, :], w_ref[...])

            def put(lo, hi, val, r0=r0):
                p_ref[r0:r0 + rows, lo:hi] = val.astype(BF16)

            epilogue(r, put, rc)

    @pl.when(j == 0)
    def _():
        def epilogue(r, put, rc):
            first = (i % per_b == 0) if rc == 0 else None
            conv = _causal_conv(r[:, :CONV_COLS], xpad_ref, cw_ref, cb_ref, first)
            put(0, SILU_CONV_END, _silu(conv[:, :SILU_CONV_END]))
            put(SILU_CONV_END, COL_BC, conv[:, SILU_CONV_END:COL_BC])
            put(COL_BC, CONV_COLS, _silu(conv[:, COL_BC:]))
            put(CONV_COLS, PROJ_TN, r[:, CONV_COLS:])

        chunked(epilogue)

    @pl.when(j == 1)
    def _():
        z0, g0 = COL_ZGDN - PROJ_TN, COL_GMERGE - PROJ_TN

        def epilogue(r, put, rc):
            put(0, z0, r[:, :z0])
            put(z0, g0, _silu(r[:, z0:g0]))
            put(g0, PROJ_TN, _sigmoid(r[:, g0:]))

        chunked(epilogue)

    @pl.when(j == 2)
    def _():
        l0, u0 = COL_GLRU - 2 * PROJ_TN, COL_US5B - 2 * PROJ_TN

        def epilogue(r, put, rc):
            put(0, l0, _sigmoid(r[:, :l0]))
            put(l0, u0, _gelu(r[:, l0:u0]))
            put(u0, PROJ_TN, r[:, u0:])

        chunked(epilogue)


def _inproj(x2, mod3, ln_g, w_main, w_small, conv_w, conv_b, seq):
    t = x2.shape[0]
    tm = min(512, seq)
    per_b = seq // tm
    return pl.pallas_call(
        functools.partial(_inproj_kernel, per_b=per_b),
        grid=(PROJ_TILES, t // tm),
        in_specs=[
            pl.BlockSpec((tm, D_MODEL), lambda j, i: (i, 0)),
            pl.BlockSpec((None, 1, 6 * D_MODEL), lambda j, i: (i // per_b, 0, 0)),
            pl.BlockSpec((1, D_MODEL), lambda j, i: (0, 0)),
            pl.BlockSpec((D_MODEL, PROJ_TN), lambda j, i: (0, j)),
            pl.BlockSpec((D_MODEL, SMALL_COLS), lambda j, i: (0, 0)),
            pl.BlockSpec((CONV_K, CONV_COLS), lambda j, i: (0, 0)),
            pl.BlockSpec((1, CONV_COLS), lambda j, i: (0, 0)),
        ],
        out_specs=[
            pl.BlockSpec((tm, PROJ_TN), lambda j, i: (i, j)),
            pl.BlockSpec((None, tm, SMALL_COLS), lambda j, i: (j, i, 0)),
        ],
        out_shape=[
            jax.ShapeDtypeStruct((t, MAIN_COLS), BF16),
            jax.ShapeDtypeStruct((PROJ_TILES, t, SMALL_COLS), F32),
        ],
        scratch_shapes=[pltpu.VMEM((tm // PROJ_ROW_CHUNKS + SUBLANES, CONV_COLS), F32)],
        compiler_params=_params("arbitrary", "arbitrary"),
        name="inproj",
    )(x2, mod3, ln_g, w_main, w_small, conv_w, conv_b)


def _s5_kernel(u0_ref, u1_ref, u2_ref, wb_ref, pneg_ref, ppos_ref, lam1_ref, tri_ref, wc_ref, d_ref, gw_ref,
               gb_ref, o_ref, carry_ref):
    h = S5_PART_STATE

    @pl.when(pl.program_id(1) == 0)
    def _():
        carry_ref[...] = jnp.zeros_like(carry_ref)

    ys = []
    for part, u_ref in enumerate((u0_ref, u1_ref, u2_ref)):
        lo = part * 2 * h
        ub = u_ref[...]
        bu = _dot(ub, wb_ref[part])
        br, bi = bu[:, :h], bu[:, h:]
        nr, ni = pneg_ref[:, lo:lo + h], pneg_ref[:, lo + h:lo + 2 * h]
        z = jnp.concatenate([nr * br - ni * bi, nr * bi + ni * br], axis=-1).astype(BF16)
        s = _dot(tri_ref[...], z)
        cr, ci = carry_ref[:, lo:lo + h], carry_ref[:, lo + h:lo + 2 * h]
        l1r, l1i = lam1_ref[:, lo:lo + h], lam1_ref[:, lo + h:lo + 2 * h]
        sr = s[:, :h] + (l1r * cr - l1i * ci)
        si = s[:, h:] + (l1r * ci + l1i * cr)
        pr, pi = ppos_ref[:, lo:lo + h], ppos_ref[:, lo + h:lo + 2 * h]
        xr = pr * sr - pi * si
        xi = pr * si + pi * sr
        ts = xr.shape[0]
        carry_ref[:, lo:lo + h] = xr[ts - 1:ts, :]
        carry_ref[:, lo + h:lo + 2 * h] = xi[ts - 1:ts, :]
        xcat = jnp.concatenate([xr, xi], axis=-1).astype(BF16)
        ys.append(_dot(xcat, wc_ref[part]) + d_ref[:, part * LANES:(part + 1) * LANES] * ub.astype(F32))
    y = _gelu(jnp.concatenate(ys, axis=-1))
    o_ref[...] = (y * _sigmoid(_dot(y.astype(BF16), gw_ref[...]) + gb_ref[...])).astype(BF16)


def _s5_tables(lam_re, lam_im, log_dt, b_re, b_im, c_re, c_im, ts):
    gp = S5_GROUPS // S5_PARTS
    dt = jnp.exp(log_dt)[:, None]
    ar, ai = lam_re * dt, lam_im * dt
    er = jnp.exp(ar)
    lbr, lbi = er * jnp.cos(ai), er * jnp.sin(ai)
    den = lam_re * lam_re + lam_im * lam_im
    qr = ((lbr - 1.0) * lam_re + lbi * lam_im) / den
    qi = (lbi * lam_re - (lbr - 1.0) * lam_im) / den
    bbr = qr[..., None] * b_re - qi[..., None] * b_im
    bbi = qr[..., None] * b_im + qi[..., None] * b_re
    eye = jnp.eye(gp, dtype=F32)

    def blockdiag_in(m):
        m = m.reshape(S5_PARTS, gp, S5_STATE, S5_GROUP)
        return jnp.einsum('qgph,gk->qghkp', m, eye).reshape(S5_PARTS, LANES, S5_PART_STATE)

    def blockdiag_out(m):
        m = m.reshape(S5_PARTS, gp, S5_GROUP, S5_STATE)
        return jnp.einsum('qghp,gk->qgpkh', m, eye).reshape(S5_PARTS, S5_PART_STATE, LANES)

    wb = jnp.concatenate([blockdiag_in(bbr), blockdiag_in(bbi)], axis=2).astype(BF16)
    wc = jnp.concatenate([blockdiag_out(c_re), -blockdiag_out(c_im)], axis=1).astype(BF16)

    def lanes(re, im):
        lead = re.shape[:-2]
        re = re.reshape(lead + (S5_PARTS, 1, S5_PART_STATE))
        im = im.reshape(lead + (S5_PARTS, 1, S5_PART_STATE))
        return jnp.concatenate([re, im], axis=-2).reshape(lead + (S5_PARTS * 2 * S5_PART_STATE,))

    j = jnp.arange(ts, dtype=F32)[:, None, None]

    def power(sign):
        mag = jnp.exp(sign * j * ar[None])
        ang = sign * j * ai[None]
        return lanes(mag * jnp.cos(ang), mag * jnp.sin(ang))

    return wb, power(-1.0), power(1.0), lanes(lbr, lbi)[None, :], wc


def _s5_mixer(p_main, bsz, seq, tables, d, glu_w, glu_b):
    ts = min(CHUNK, seq)
    wb, pneg, ppos, lam1, wc = tables
    tri = jnp.tril(jnp.ones((ts, ts), F32)).astype(BF16)
    nsb = seq // ts
    consts = [wb, pneg, ppos, lam1, tri, wc, d.reshape(1, -1), glu_w.astype(BF16), glu_b.reshape(1, -1)]
    ucols = (COL_US5A // LANES, COL_US5A // LANES + 1, COL_US5B // LANES)
    return pl.pallas_call(
        _s5_kernel,
        grid=(bsz, nsb),
        in_specs=[pl.BlockSpec((ts, LANES), functools.partial(lambda b, s, col: (b * nsb + s, col), col=col))
                  for col in ucols] + [_full(a.shape) for a in consts],
        out_specs=pl.BlockSpec((ts, S5_WIDTH), lambda b, s: (b * nsb + s, 0)),
        out_shape=jax.ShapeDtypeStruct((bsz * seq, S5_WIDTH), BF16),
        scratch_shapes=[pltpu.VMEM((1, S5_PARTS * 2 * S5_PART_STATE), F32)],
        compiler_params=_params("parallel", "arbitrary"),
        name="s5_mixer",
    )(p_main, p_main, p_main, *consts)


INV_BASE = 16


def _block_inverse_rows(mt):
    groups, n, w = mt.shape
    mt2 = mt.reshape(groups * n, w)
    sub = lax.broadcasted_iota(jnp.int32, (groups * n, w), 0) % n
    lane = lax.broadcasted_iota(jnp.int32, (groups * n, w), 1)
    lane_in = lane % n
    lane_base = lane - lane_in
    x = (sub == lane_in).astype(F32)
    for i in range(1, n):
        coef = jnp.take_along_axis(mt2, lane_base + i, axis=1)
        acc = jnp.sum((coef * x).reshape(groups, n, w), axis=1, keepdims=True)
        new_row = jnp.where(lane_in[0:1, :] == i, 1.0, 0.0) - acc
        new_rows = jnp.broadcast_to(new_row, (groups, n, w)).reshape(groups * n, w)
        x = jnp.where(sub == i, new_rows, x)
    return x.reshape(groups, n, w)


def _unit_lower_inverse(a, at, ri, ci):
    n = a.shape[1]
    assert n == LANES
    diag_blk = ri // INV_BASE == ci // INV_BASE
    at_blk = jnp.where(diag_blk, at, 0.0)
    mt = at_blk[:, 0:INV_BASE, :]
    for blk in range(1, n // INV_BASE):
        mt = mt + at_blk[:, blk * INV_BASE:(blk + 1) * INV_BASE, :]
    x = _block_inverse_rows(mt)
    inv = jnp.where(diag_blk, jnp.concatenate([x] * (n // INV_BASE), axis=1), 0.0)
    size = INV_BASE
    while size < n:
        lower_left = (ri // (2 * size) == ci // (2 * size)) & ((ri // size) % 2 == 1) & ((ci // size) % 2 == 0)
        invb = inv.astype(BF16)
        inv = inv - _bmm(_bmm(invb, jnp.where(lower_left, a, 0.0).astype(BF16)).astype(BF16), invb)
        size *= 2
    return inv


def _gdn_kernel(qkv_ref, z_ref, small_ref, prm_ref, nw_ref, tri_ref, o_ref, state_ref):
    @pl.when(pl.program_id(1) == 0)
    def _():
        state_ref[...] = jnp.zeros_like(state_ref)

    ts = qkv_ref.shape[0]
    sm = small_ref[...]
    beta_all = _sigmoid(sm)
    g_all = -jnp.exp(prm_ref[0:1, :]) * _softplus(sm + prm_ref[1:2, :])
    gc = _select_rows(tri_ref[...], g_all)
    small_lane = lax.broadcasted_iota(jnp.int32, sm.shape, 1)
    rows_t = jnp.where(small_lane < SMALL_A, beta_all, gc).T
    nchunk = ts // CHUNK
    ri = lax.broadcasted_iota(jnp.int32, (CHUNK, CHUNK), 0)
    ci = lax.broadcasted_iota(jnp.int32, (CHUNK, CHUNK), 1)
    groups = [(c, hd) for c in range(nchunk) for hd in range(GDN_HEADS)]

    def head_tiles(off):
        return jnp.stack([qkv_ref[c * CHUNK:(c + 1) * CHUNK, off + hd * GDN_DK:off + (hd + 1) * GDN_DK]
                          for c, hd in groups]).astype(F32)

    def head_cols(arr, lane0):
        return jnp.stack([arr[c * CHUNK:(c + 1) * CHUNK, lane0 + hd:lane0 + hd + 1] for c, hd in groups])

    def head_rows(lane0):
        return jnp.stack([rows_t[lane0 + hd:lane0 + hd + 1, c * CHUNK:(c + 1) * CHUNK] for c, hd in groups])

    q = head_tiles(0)
    k = head_tiles(GDN_QK)
    v = head_tiles(2 * GDN_QK)
    q = q * lax.rsqrt(_lane_sum(q * q) + NORM_EPS) * (GDN_DK ** -0.5)
    k = k * lax.rsqrt(_lane_sum(k * k) + NORM_EPS)
    col, row = head_cols(gc, SMALL_A), head_rows(SMALL_A)
    beta, beta_row = head_cols(beta_all, SMALL_B), head_rows(SMALL_B)
    decay = jnp.exp(jnp.where(ri >= ci, col - row, -jnp.inf))
    decay_t = jnp.exp(jnp.where(ci >= ri, row - col, -jnp.inf))
    kb = k * beta
    kbf = k.astype(BF16)
    kk = _bmm_nt(kbf, kbf)
    a = jnp.where(ri > ci, kk * decay, 0.0) * beta
    at = jnp.where(ci > ri, kk * decay_t, 0.0) * beta_row
    inv = _unit_lower_inverse(a, at, ri, ci)
    eg = jnp.exp(col)
    rhs = jnp.concatenate([kb * eg, v * beta], axis=-1).astype(BF16)
    sol = _bmm(inv.astype(BF16), rhs)
    w, u = sol[:, :, :GDN_DK].astype(BF16), sol[:, :, GDN_DK:]
    attn = (_bmm_nt(q.astype(BF16), kbf) * decay).astype(BF16)
    g_last = col[:, CHUNK - 1:CHUNK, :]
    q_dec = (q * eg).astype(BF16)
    k_dec = (k * jnp.exp(g_last - col)).astype(BF16)
    e_last = jnp.exp(g_last)
    st = state_ref[...]
    for c in range(nchunk):
        sl = slice(c * GDN_HEADS, (c + 1) * GDN_HEADS)
        stb = st.astype(BF16)
        v_new = u[sl] - _bmm(w[sl], stb)
        vnb = v_new.astype(BF16)
        o = _bmm(q_dec[sl], stb) + _bmm(attn[sl], vnb)
        st = st * e_last[sl] + jnp.einsum('hik,hiv->hkv', k_dec[sl], vnb, preferred_element_type=F32)
        o = o * lax.rsqrt(_lane_sum(o * o) * (1.0 / GDN_DV) + NORM_EPS)
        for hd in range(GDN_HEADS):
            gate = z_ref[c * CHUNK:(c + 1) * CHUNK, hd * GDN_DV:(hd + 1) * GDN_DV].astype(F32)
            o_ref[c * CHUNK:(c + 1) * CHUNK, hd * GDN_DV:(hd + 1) * GDN_DV] = (
                o[hd] * nw_ref[...] * gate).astype(BF16)
    state_ref[...] = st


def _pad_lanes(v, start):
    return jnp.zeros((SMALL_COLS,), F32).at[start:start + v.shape[0]].set(v)


def _gdn_mixer(p_main, small, bsz, seq, a_log, dt_bias, norm_w):
    ts = min(GDN_TILE, seq)
    nsb = seq // ts
    width = 3 * GDN_QK
    prm = jnp.stack([_pad_lanes(a_log, SMALL_A), _pad_lanes(dt_bias, SMALL_A)])
    tri = jnp.kron(jnp.eye(ts // CHUNK, dtype=F32), jnp.tril(jnp.ones((CHUNK, CHUNK), F32))).astype(BF16)
    consts = [prm, norm_w.reshape(1, -1), tri]
    return pl.pallas_call(
        _gdn_kernel,
        grid=(bsz, nsb),
        in_specs=[
            pl.BlockSpec((ts, width), lambda b, s: (b * nsb + s, COL_QKV // width)),
            pl.BlockSpec((ts, GDN_WIDTH), lambda b, s: (b * nsb + s, COL_ZGDN // GDN_WIDTH)),
            pl.BlockSpec((None, ts, SMALL_COLS), lambda b, s: (0, b * nsb + s, 0)),
        ] + [_full(a.shape) for a in consts],
        out_specs=pl.BlockSpec((ts, GDN_WIDTH), lambda b, s: (b * nsb + s, 0)),
        out_shape=jax.ShapeDtypeStruct((bsz * seq, GDN_WIDTH), BF16),
        scratch_shapes=[pltpu.VMEM((GDN_HEADS, GDN_DK, GDN_DV), F32)],
        compiler_params=_params("parallel", "arbitrary"),
        name="gdn_mixer",
    )(p_main, p_main, small, *consts)


def _ssd_kernel(x_ref, z_ref, bc_ref, small_ref, prm_ref, hv_ref, tri_ref, expand_ref, smask_ref, o_ref,
                state_ref):
    @pl.when(pl.program_id(1) == 0)
    def _():
        state_ref[...] = jnp.zeros_like(state_ref)

    x = x_ref[...].astype(F32)
    ts = x.shape[0]
    bmb, cmb = bc_ref[:, :SSD_BC], bc_ref[:, SSD_BC:]
    cm = cmb.astype(F32)
    sm = small_ref[...]
    dt_all = _softplus(sm + prm_ref[1:2, :])
    la_all = dt_all * (-jnp.exp(prm_ref[0:1, :]))
    cs = _select_rows(tri_ref[...], la_all)
    cst = cs.T
    expand = expand_ref[...]
    dtx = _select_cols(dt_all, expand)
    csx = _select_cols(cs, expand)
    cs_last_x = csx[ts - 1:ts, :]
    xdt = x * dtx
    xdtb = xdt.astype(BF16)
    ri = lax.broadcasted_iota(jnp.int32, (ts, ts), 0)
    ci = lax.broadcasted_iota(jnp.int32, (ts, ts), 1)
    lane = lax.broadcasted_iota(jnp.int32, (ts, SSD_BC), 1)
    cb = []
    for g in range(SSD_GROUPS):
        in_g = (lane >= g * SSD_STATE) & (lane < (g + 1) * SSD_STATE)
        cb.append(_dot_nt(jnp.where(in_g, cm, 0.0).astype(BF16), bmb))
    halves = []
    for pair in range(SSD_HEADS // 2):
        rhs = xdtb[:, pair * LANES:(pair + 1) * LANES]
        ys = []
        for hd in (2 * pair, 2 * pair + 1):
            col = cs[:, SMALL_DT + hd:SMALL_DT + hd + 1]
            row = cst[SMALL_DT + hd:SMALL_DT + hd + 1, :]
            decay = jnp.exp(jnp.where(ri >= ci, col - row, -jnp.inf))
            ys.append(_dot((cb[hd // SSD_HPG] * decay).astype(BF16), rhs))
        halves.append(jnp.where(lane < SSD_HEAD_DIM, ys[0], ys[1]))
    y = jnp.concatenate(halves, axis=-1)
    st = state_ref[...]
    y = y + _dot(cmb, st.astype(BF16)) * jnp.exp(csx)
    to_end = jnp.exp(cs_last_x - csx)
    upd = _dot_tn(bmb, (xdt * to_end).astype(BF16))
    state_ref[...] = st * jnp.exp(cs_last_x) + upd * smask_ref[...]
    y = y + hv_ref[0:1, :] * x
    y = y * z_ref[...].astype(F32)
    y = y * lax.rsqrt(jnp.mean(y * y, axis=-1, keepdims=True) + NORM_EPS)
    o_ref[...] = (y * hv_ref[1:2, :]).astype(BF16)


def _ssd_mixer(p_main, small, bsz, seq, a_log, dt_bias, d, norm_w):
    ts = min(CHUNK, seq)
    nsb = seq // ts
    prm = jnp.stack([_pad_lanes(a_log, SMALL_DT), _pad_lanes(dt_bias, SMALL_DT)])
    hv = jnp.stack([jnp.repeat(d, SSD_HEAD_DIM), norm_w])
    tri = jnp.tril(jnp.ones((ts, ts), F32)).astype(BF16)
    lane_head = jnp.arange(SSD_WIDTH) // SSD_HEAD_DIM
    expand = (jnp.arange(SMALL_COLS)[:, None] == SMALL_DT + lane_head[None, :]).astype(BF16)
    row_group = jnp.arange(SSD_BC) // SSD_STATE
    smask = (row_group[:, None] == (lane_head // SSD_HPG)[None, :]).astype(F32)
    consts = [prm, hv, tri, expand, smask]
    bc_w = 2 * SSD_BC
    return pl.pallas_call(
        _ssd_kernel,
        grid=(bsz, nsb),
        in_specs=[
            pl.BlockSpec((ts, SSD_WIDTH), lambda b, s: (b * nsb + s, COL_XSSD // SSD_WIDTH)),
            pl.BlockSpec((ts, SSD_WIDTH), lambda b, s: (b * nsb + s, COL_ZSSD // SSD_WIDTH)),
            pl.BlockSpec((ts, bc_w), lambda b, s: (b * nsb + s, COL_BC // bc_w)),
            pl.BlockSpec((None, ts, SMALL_COLS), lambda b, s: (0, b * nsb + s, 0)),
        ] + [_full(a.shape) for a in consts],
        out_specs=pl.BlockSpec((ts, SSD_WIDTH), lambda b, s: (b * nsb + s, 0)),
        out_shape=jax.ShapeDtypeStruct((bsz * seq, SSD_WIDTH), BF16),
        scratch_shapes=[pltpu.VMEM((SSD_BC, SSD_WIDTH), F32)],
        compiler_params=_params("parallel", "arbitrary"),
        name="ssd_mixer",
    )(p_main, p_main, p_main, small, *consts)


def _lru_kernel(x_ref, g_ref, wr_ref, wi_ref, vec_ref, o_ref, h_ref):
    step = pl.program_id(1)

    @pl.when(step == 0)
    def _():
        h_ref[...] = jnp.zeros_like(h_ref)

    xb = x_ref[...]
    x = xb.astype(F32)
    ts = x.shape[0]
    r = _sigmoid(_dot(xb, wr_ref[...]) + vec_ref[0:1, :])
    gate_i = _sigmoid(_dot(xb, wi_ref[...]) + vec_ref[1:2, :])
    log_a = -LRU_C * r * _softplus(-vec_ref[2:3, :])
    a = jnp.exp(log_a)
    mult = jnp.sqrt(-jnp.tanh(log_a) * (a * a + 1.0))
    row = lax.broadcasted_iota(jnp.int32, (ts, LRU_WIDTH), 0)
    mult = jnp.where(row + step * ts == 0, 1.0, mult)
    b = mult * gate_i * x
    shift = 1
    while shift < ts:
        keep = row >= shift
        a_prev = jnp.where(keep, pltpu.roll(a, shift, 0), 1.0)
        b_prev = jnp.where(keep, pltpu.roll(b, shift, 0), 0.0)
        b = a * b_prev + b
        a = a * a_prev
        shift *= 2
    h = b + a * h_ref[...]
    h_ref[...] = h[ts - 1:ts, :]
    o_ref[...] = (h * g_ref[...].astype(F32)).astype(BF16)


def _lru_mixer(p_main, bsz, seq, lam, wr, br, wi, bi):
    ts = min(LRU_TILE, seq)
    nsb = seq // ts
    eye = jnp.eye(LRU_BLOCKS, dtype=F32)

    def blockdiag(w):
        return jnp.einsum('nde,nm->ndme', w, eye).reshape(LRU_WIDTH, LRU_WIDTH).astype(BF16)

    consts = [blockdiag(wr), blockdiag(wi), jnp.stack([br, bi, lam])]
    return pl.pallas_call(
        _lru_kernel,
        grid=(bsz, nsb),
        in_specs=[
            pl.BlockSpec((ts, LRU_WIDTH), lambda b, s: (b * nsb + s, COL_XLRU // LRU_WIDTH)),
            pl.BlockSpec((ts, LRU_WIDTH), lambda b, s: (b * nsb + s, COL_GLRU // LRU_WIDTH)),
        ] + [_full(a.shape) for a in consts],
        out_specs=pl.BlockSpec((ts, LRU_WIDTH), lambda b, s: (b * nsb + s, 0)),
        out_shape=jax.ShapeDtypeStruct((bsz * seq, LRU_WIDTH), BF16),
        scratch_shapes=[pltpu.VMEM((1, LRU_WIDTH), F32)],
        compiler_params=_params("parallel", "arbitrary"),
        name="lru_mixer",
    )(p_main, p_main, *consts)


def _merge_kernel(ya_ref, yb_ref, yc_ref, yd_ref, gm_ref, x_ref, mod_ref, wa_ref, wb_ref, wc_ref, wd_ref,
                  wo_ref, o_ref):
    merged = None
    for i, (y_ref, w_ref) in enumerate(((ya_ref, wa_ref), (yb_ref, wb_ref), (yc_ref, wc_ref), (yd_ref, wd_ref))):
        gate = gm_ref[:, i * D_MODEL:(i + 1) * D_MODEL].astype(F32)
        term = gate * _dot(y_ref[...], w_ref[...])
        merged = term if merged is None else merged + term
    out = _dot(merged.astype(BF16), wo_ref[...])
    o_ref[...] = x_ref[...] + mod_ref[:, 2 * D_MODEL:3 * D_MODEL] * out


def _merge(ys, p_main, x2, mod3, w_branch, w_out, seq):
    t = x2.shape[0]
    tm = min(512, seq)
    per_b = seq // tm
    widths = (S5_WIDTH, GDN_WIDTH, SSD_WIDTH, LRU_WIDTH)
    rows, acc = [], 0
    for wd in widths:
        rows.append(w_branch[acc:acc + wd].astype(BF16))
        acc += wd
    gw = N_BRANCH * D_MODEL
    return pl.pallas_call(
        _merge_kernel,
        grid=(t // tm,),
        in_specs=[pl.BlockSpec((tm, wd), lambda i: (i, 0)) for wd in widths] + [
            pl.BlockSpec((tm, gw), lambda i: (i, COL_GMERGE // gw)),
            pl.BlockSpec((tm, D_MODEL), lambda i: (i, 0)),
            pl.BlockSpec((None, 1, 6 * D_MODEL), lambda i: (i // per_b, 0, 0)),
        ] + [_full(r.shape) for r in rows] + [_full((D_MODEL, D_MODEL))],
        out_specs=pl.BlockSpec((tm, D_MODEL), lambda i: (i, 0)),
        out_shape=jax.ShapeDtypeStruct((t, D_MODEL), F32),
        compiler_params=_params("parallel"),
        name="merge_out",
    )(*ys, p_main, x2, mod3, *rows, w_out.astype(BF16))


FFN_SLICES = tuple((s, min(512, FFN_HIDDEN - s)) for s in range(0, FFN_HIDDEN, 512))


def _ffn_kernel(x_ref, mod_ref, g_ref, w13_ref, w2_ref, gf_ref, o_ref, *, final):
    x = x_ref[...]
    h = _modulated_norm(x, g_ref[...], mod_ref[:, 3 * D_MODEL:4 * D_MODEL], mod_ref[:, 4 * D_MODEL:5 * D_MODEL])
    hb = h.astype(BF16)
    acc = None
    for start, size in FFN_SLICES:
        a = _dot(hb, w13_ref[:, start:start + size])
        b = _dot(hb, w13_ref[:, FFN_HIDDEN + start:FFN_HIDDEN + start + size])
        part = _dot((_silu(a) * b).astype(BF16), w2_ref[start:start + size, :])
        acc = part if acc is None else acc + part
    x = x + mod_ref[:, 5 * D_MODEL:6 * D_MODEL] * acc
    if final:
        x = x * lax.rsqrt(jnp.mean(x * x, axis=-1, keepdims=True) + NORM_EPS) * gf_ref[...]
    o_ref[...] = x


def _ffn(x2, mod3, ln_g, w13, w2, ln_final, seq, final):
    t = x2.shape[0]
    tm = min(512, seq)
    per_b = seq // tm
    return pl.pallas_call(
        functools.partial(_ffn_kernel, final=final),
        grid=(t // tm,),
        in_specs=[
            pl.BlockSpec((tm, D_MODEL), lambda i: (i, 0)),
            pl.BlockSpec((None, 1, 6 * D_MODEL), lambda i: (i // per_b, 0, 0)),
            _full((1, D_MODEL)),
            pl.BlockSpec((D_MODEL, 2 * FFN_HIDDEN), lambda i: (0, 0), pipeline_mode=pl.Buffered(1)),
            pl.BlockSpec((FFN_HIDDEN, D_MODEL), lambda i: (0, 0), pipeline_mode=pl.Buffered(1)),
            _full((1, D_MODEL)),
        ],
        out_specs=pl.BlockSpec((tm, D_MODEL), lambda i: (i, 0)),
        out_shape=jax.ShapeDtypeStruct((t, D_MODEL), F32),
        compiler_params=_params("parallel"),
        name="ffn_final" if final else "ffn",
    )(x2, mod3, ln_g, w13, w2, ln_final)


def _split_in_weights(w_in, conv_w, conv_b):
    o = 0
    seg = {}
    for name, width in (("q", 512), ("k", 512), ("v", 512), ("xssd", 512), ("bssd", 128), ("cssd", 128),
                        ("xlru", 512), ("us5", 384), ("bgdn", 4), ("agdn", 4), ("zgdn", 512), ("zssd", 512),
                        ("dtssd", 8), ("glru", 512), ("gmerge", 4096)):
        seg[name] = (o, o + width)
        o += width

    def cols(arr, names):
        return jnp.concatenate([arr[:, seg[n][0]:seg[n][1]] for n in names], axis=1)

    conv_names = ("q", "k", "v", "xssd", "xlru", "bssd", "cssd")
    us5 = w_in[:, seg["us5"][0]:seg["us5"][1]]
    main = jnp.concatenate([cols(w_in, conv_names), us5[:, :2 * LANES], cols(w_in, ("zgdn", "zssd", "gmerge", "glru")),
                            us5[:, 2 * LANES:]], axis=1).astype(BF16)
    pad = jnp.zeros((w_in.shape[0], SMALL_COLS - 16), w_in.dtype)
    small = jnp.concatenate([cols(w_in, ("bgdn", "agdn", "dtssd")), pad], axis=1).astype(BF16)
    return main, small, cols(conv_w, conv_names), cols(conv_b.reshape(1, -1), conv_names)


def kernel(x, c, ln_mix_g, ln_ffn_g, ln_final_g, ada_w, ada_b, w_in, conv_w, conv_b, s5_lambda_re, s5_lambda_im, s5_log_dt, s5_b_re, s5_b_im, s5_c_re, s5_c_im, s5_d, s5_glu_w, s5_glu_b, gdn_a_log, gdn_dt_bias, gdn_norm_w, ssd_a_log, ssd_dt_bias, ssd_d, ssd_norm_w, lru_lambda, lru_wr, lru_br, lru_wi, lru_bi, w_branch, w_out, ffn_w13, ffn_w2):
    bsz, seq, d = x.shape
    t = bsz * seq
    x2 = x.reshape(t, d)
    mod = _ada_mod(c, ada_w, ada_b)
    ln_final = ln_final_g.reshape(1, d)
    for l in range(DEPTH):
        mod3 = mod[l].reshape(bsz, 1, 6 * d)
        w_main, w_small, cw, cb = _split_in_weights(w_in[l], conv_w[l], conv_b[l])
        p_main, small = _inproj(x2, mod3, ln_mix_g[l].reshape(1, d), w_main, w_small, cw, cb, seq)

        tables = _s5_tables(s5_lambda_re[l], s5_lambda_im[l], s5_log_dt[l], s5_b_re[l], s5_b_im[l],
                            s5_c_re[l], s5_c_im[l], min(CHUNK, seq))
        y_a = _s5_mixer(p_main, bsz, seq, tables, s5_d[l], s5_glu_w[l], s5_glu_b[l])
        y_b = _gdn_mixer(p_main, small, bsz, seq, gdn_a_log[l], gdn_dt_bias[l], gdn_norm_w[l])
        y_c = _ssd_mixer(p_main, small, bsz, seq, ssd_a_log[l], ssd_dt_bias[l], ssd_d[l], ssd_norm_w[l])
        y_d = _lru_mixer(p_main, bsz, seq, lru_lambda[l], lru_wr[l], lru_br[l], lru_wi[l], lru_bi[l])
        x2 = _merge((y_a, y_b, y_c, y_d), p_main, x2, mod3, w_branch[l], w_out[l], seq)
        x2 = _ffn(x2, mod3, ln_ffn_g[l].reshape(1, d), ffn_w13[l].astype(BF16), ffn_w2[l].astype(BF16),
                  ln_final, seq, final=(l == DEPTH - 1))
    return x2.reshape(bsz, seq, d)
```

```python
import functools

import jax
import jax.numpy as jnp
from jax import lax
from jax.experimental import pallas as pl
from jax.experimental.pallas import tpu as pltpu

F32 = jnp.float32
BF16 = jnp.bfloat16

D_MODEL = 1024
DEPTH = 2
CONV_K = 4
NORM_EPS = 1e-6
S5_GROUP, S5_STATE, S5_WIDTH, S5_GROUPS = 16, 64, 384, 24
GDN_HEADS, GDN_DK, GDN_DV = 4, 128, 128
GDN_QK = GDN_HEADS * GDN_DK
GDN_WIDTH = GDN_HEADS * GDN_DV
SSD_HEAD_DIM, SSD_WIDTH, SSD_HEADS, SSD_GROUPS, SSD_HPG, SSD_STATE = 64, 512, 8, 2, 4, 64
SSD_BC = SSD_GROUPS * SSD_STATE
LRU_WIDTH, LRU_BLOCK, LRU_BLOCKS, LRU_C = 512, 64, 8, 8.0
N_BRANCH = 4
FFN_HIDDEN = 2816

LANES = 128
SUBLANES = 8

COL_QKV, COL_XSSD, COL_XLRU, COL_BC = 0, 1536, 2048, 2560
CONV_COLS = 2816
SILU_CONV_END = 2048
COL_US5A, COL_ZGDN, COL_ZSSD, COL_GMERGE, COL_GLRU, COL_US5B = 2816, 3072, 3584, 4096, 8192, 8704
MAIN_COLS = 8832
PROJ_TILES = 3
PROJ_TN = MAIN_COLS // PROJ_TILES
PROJ_ROW_CHUNKS = 2
SMALL_COLS = 128
SMALL_B, SMALL_A, SMALL_DT = 0, 4, 8

CHUNK = 128
GDN_TILE = 256
LRU_TILE = 256
S5_TILE = 256
S5_PARTS = S5_WIDTH // LANES
S5_PART_STATE = (S5_GROUPS // S5_PARTS) * S5_STATE
VMEM_LIMIT = 56 * 1024 * 1024


def _sigmoid(x):
    return 0.5 * jnp.tanh(0.5 * x) + 0.5


def _silu(x):
    return x * _sigmoid(x)


def _softplus(x):
    return jnp.maximum(x, 0.0) + jnp.log1p(jnp.exp(-jnp.abs(x)))


def _gelu(x):
    return jax.nn.gelu(x, approximate=True)


def _dot(a, b):
    return jnp.dot(a, b, preferred_element_type=F32)


def _dot_nt(a, b):
    return lax.dot_general(a, b, (((1,), (1,)), ((), ())), preferred_element_type=F32)


def _dot_tn(a, b):
    return lax.dot_general(a, b, (((0,), (0,)), ((), ())), preferred_element_type=F32)


def _bmm(a, b):
    return jnp.einsum('gij,gjk->gik', a, b, preferred_element_type=F32)


def _bmm_nt(a, b):
    return jnp.einsum('gik,gjk->gij', a, b, preferred_element_type=F32)


def _split3(x):
    x1 = x.astype(BF16)
    r1 = x - x1.astype(F32)
    x2 = r1.astype(BF16)
    x3 = (r1 - x2.astype(F32)).astype(BF16)
    return x1, x2, x3


def _select_rows(sel_bf16, x):
    x1, x2, x3 = _split3(x)
    return _dot(sel_bf16, x1) + (_dot(sel_bf16, x2) + _dot(sel_bf16, x3))


def _select_cols(x, sel_bf16):
    x1, x2, x3 = _split3(x)
    return _dot(x1, sel_bf16) + (_dot(x2, sel_bf16) + _dot(x3, sel_bf16))


def _lane_sum(x):
    shape = x.shape
    ones = jnp.ones((shape[-1], shape[-1]), BF16)
    return _dot(x.reshape(-1, shape[-1]).astype(BF16), ones).reshape(shape)


def _params(*sem):
    return pltpu.CompilerParams(dimension_semantics=sem, vmem_limit_bytes=VMEM_LIMIT)


def _full(shape):
    n = len(shape)
    return pl.BlockSpec(shape, lambda *_: (0,) * n)


def _ada_kernel(c_ref, w_ref, b_ref, o_ref):
    cond = _silu(c_ref[...]).astype(BF16)
    o_ref[...] = _dot(cond, w_ref[...].astype(BF16)) + b_ref[...]


def _ada_mod(c, ada_w, ada_b):
    depth, d, n = ada_w.shape
    bsz = c.shape[0]
    tn = 1536
    return pl.pallas_call(
        _ada_kernel,
        grid=(depth, n // tn),
        in_specs=[
            pl.BlockSpec((bsz, d), lambda l, j: (0, 0)),
            pl.BlockSpec((None, d, tn), lambda l, j: (l, 0, j)),
            pl.BlockSpec((None, 1, tn), lambda l, j: (l, 0, j)),
        ],
        out_specs=pl.BlockSpec((None, bsz, tn), lambda l, j: (l, 0, j)),
        out_shape=jax.ShapeDtypeStruct((depth, bsz, n), F32),
        compiler_params=_params("parallel", "parallel"),
        name="ada_mod",
    )(c, ada_w, ada_b.reshape(depth, 1, n))


def _modulated_norm(x, g, shift, scale):
    ms = jnp.mean(x * x, axis=-1, keepdims=True)
    return x * lax.rsqrt(ms + NORM_EPS) * g * (1.0 + scale) + shift


CONV_STRIDE = 4


def _causal_conv(x, xpad_ref, stage_ref, w_ref, b_ref, emit):
    ts, c = x.shape
    span = CONV_STRIDE * SUBLANES

    for tile in range(c // LANES):
        lanes = slice(tile * LANES, (tile + 1) * LANES)
        xt = x[:, lanes]
        xpad_ref[tile, SUBLANES:SUBLANES + ts, :] = xt
        taps = [jnp.broadcast_to(w_ref[k:k + 1, lanes], (SUBLANES, LANES)) for k in range(CONV_K)]
        bias = jnp.broadcast_to(b_ref[:, lanes], (SUBLANES, LANES))
        for group in range(ts // span):
            for res in range(CONV_STRIDE):
                row0 = group * span + res
                acc = bias
                for k in range(CONV_K):
                    start = SUBLANES + row0 - (CONV_K - 1) + k
                    acc = acc + taps[k] * xpad_ref[tile, pl.ds(start, SUBLANES, stride=CONV_STRIDE), :]
                stage_ref[tile, pl.ds(row0, SUBLANES, stride=CONV_STRIDE), :] = acc
        xpad_ref[tile, 0:SUBLANES, :] = xt[ts - SUBLANES:, :]
        emit(tile, stage_ref[tile])


def _inproj_kernel(x_ref, mod_ref, g_ref, w_ref, ws_ref, cw_ref, cb_ref, p_ref, small_ref, xpad_ref, stage_ref, *,
                   per_b):
    rows = x_ref.shape[0] // PROJ_ROW_CHUNKS

    @pl.when(pl.program_id(0) % per_b == 0)
    def _():
        xpad_ref[:, 0:SUBLANES, :] = jnp.zeros((CONV_COLS // LANES, SUBLANES, LANES), F32)

    def column_tile_0(r, put):
        def emit(tile, y):
            lo = tile * LANES
            plain = SILU_CONV_END <= lo < COL_BC
            put(lo, lo + LANES, y if plain else _silu(y))

        _causal_conv(r[:, :CONV_COLS], xpad_ref, stage_ref, cw_ref, cb_ref, emit)
        put(CONV_COLS, PROJ_TN, r[:, CONV_COLS:])

    def column_tile_1(r, put):
        z0, g0 = COL_ZGDN - PROJ_TN, COL_GMERGE - PROJ_TN
        put(PROJ_TN, PROJ_TN + z0, r[:, :z0])
        put(PROJ_TN + z0, PROJ_TN + g0, _silu(r[:, z0:g0]))
        put(PROJ_TN + g0, 2 * PROJ_TN, _sigmoid(r[:, g0:]))

    def column_tile_2(r, put):
        l0, u0 = COL_GLRU - 2 * PROJ_TN, COL_US5B - 2 * PROJ_TN
        put(2 * PROJ_TN, 2 * PROJ_TN + l0, _sigmoid(r[:, :l0]))
        put(2 * PROJ_TN + l0, 2 * PROJ_TN + u0, _gelu(r[:, l0:u0]))
        put(2 * PROJ_TN + u0, 3 * PROJ_TN, r[:, u0:])

    for rc in range(PROJ_ROW_CHUNKS):
        r0 = rc * rows
        h = _modulated_norm(x_ref[r0:r0 + rows, :], g_ref[...], mod_ref[:, 0:D_MODEL],
                            mod_ref[:, D_MODEL:2 * D_MODEL])
        hb = h.astype(BF16)
        small_ref[r0:r0 + rows, :] = _dot(hb, ws_ref[...])

        def put(lo, hi, val, r0=r0):
            p_ref[r0:r0 + rows, lo:hi] = val.astype(BF16)

        for jt, epilogue in enumerate((column_tile_0, column_tile_1, column_tile_2)):
            epilogue(_dot(hb, w_ref[:, jt * PROJ_TN:(jt + 1) * PROJ_TN]), put)


def _inproj(x2, mod3, ln_g, w_main, w_small, conv_w, conv_b, seq):
    t = x2.shape[0]
    tm = min(512, seq)
    per_b = seq // tm
    return pl.pallas_call(
        functools.partial(_inproj_kernel, per_b=per_b),
        grid=(t // tm,),
        in_specs=[
            pl.BlockSpec((tm, D_MODEL), lambda i: (i, 0)),
            pl.BlockSpec((None, 1, 6 * D_MODEL), lambda i: (i // per_b, 0, 0)),
            pl.BlockSpec((1, D_MODEL), lambda i: (0, 0)),
            pl.BlockSpec((D_MODEL, MAIN_COLS), lambda i: (0, 0), pipeline_mode=pl.Buffered(1)),
            pl.BlockSpec((D_MODEL, SMALL_COLS), lambda i: (0, 0)),
            pl.BlockSpec((CONV_K, CONV_COLS), lambda i: (0, 0)),
            pl.BlockSpec((1, CONV_COLS), lambda i: (0, 0)),
        ],
        out_specs=[
            pl.BlockSpec((tm, MAIN_COLS), lambda i: (i, 0)),
            pl.BlockSpec((tm, SMALL_COLS), lambda i: (i, 0)),
        ],
        out_shape=[
            jax.ShapeDtypeStruct((t, MAIN_COLS), BF16),
            jax.ShapeDtypeStruct((t, SMALL_COLS), F32),
        ],
        scratch_shapes=[pltpu.VMEM((CONV_COLS // LANES, tm // PROJ_ROW_CHUNKS + SUBLANES, LANES), F32),
                        pltpu.VMEM((CONV_COLS // LANES, tm // PROJ_ROW_CHUNKS, LANES), F32)],
        compiler_params=_params("arbitrary"),
        name="inproj",
    )(x2, mod3, ln_g, w_main, w_small, conv_w, conv_b)


def _s5_kernel(u0_ref, u1_ref, u2_ref, wb_ref, pneg_ref, ppos_ref, lam1_ref, tri_ref, wc_ref, d_ref, gw_ref,
               gb_ref, o_ref, carry_ref):
    h = S5_PART_STATE

    @pl.when(pl.program_id(1) == 0)
    def _():
        carry_ref[...] = jnp.zeros_like(carry_ref)

    nchunk = u0_ref.shape[0] // CHUNK
    tri = tri_ref[...]
    ys = [[None] * S5_PARTS for _ in range(nchunk)]
    for part, u_ref in enumerate((u0_ref, u1_ref, u2_ref)):
        lo = part * 2 * h
        ub = u_ref[...]
        bu = _dot(ub, wb_ref[part])
        nr, ni = pneg_ref[:, lo:lo + h], pneg_ref[:, lo + h:lo + 2 * h]
        sums = []
        for c in range(nchunk):
            br, bi = bu[c * CHUNK:(c + 1) * CHUNK, :h], bu[c * CHUNK:(c + 1) * CHUNK, h:]
            z = jnp.concatenate([nr * br - ni * bi, nr * bi + ni * br], axis=-1).astype(BF16)
            sums.append(_dot(tri, z))
        cr, ci = carry_ref[:, lo:lo + h], carry_ref[:, lo + h:lo + 2 * h]
        l1r, l1i = lam1_ref[:, lo:lo + h], lam1_ref[:, lo + h:lo + 2 * h]
        pr, pi = ppos_ref[:, lo:lo + h], ppos_ref[:, lo + h:lo + 2 * h]
        for c in range(nchunk):
            s = sums[c]
            sr = s[:, :h] + (l1r * cr - l1i * ci)
            si = s[:, h:] + (l1r * ci + l1i * cr)
            xr = pr * sr - pi * si
            xi = pr * si + pi * sr
            cr, ci = xr[CHUNK - 1:CHUNK, :], xi[CHUNK - 1:CHUNK, :]
            xcat = jnp.concatenate([xr, xi], axis=-1).astype(BF16)
            skip = d_ref[:, part * LANES:(part + 1) * LANES] * ub[c * CHUNK:(c + 1) * CHUNK, :].astype(F32)
            ys[c][part] = _dot(xcat, wc_ref[part]) + skip
        carry_ref[:, lo:lo + h] = cr
        carry_ref[:, lo + h:lo + 2 * h] = ci
    y = _gelu(jnp.concatenate([jnp.concatenate(row, axis=-1) for row in ys], axis=0))
    o_ref[...] = (y * _sigmoid(_dot(y.astype(BF16), gw_ref[...]) + gb_ref[...])).astype(BF16)


def _s5_tables(lam_re, lam_im, log_dt, b_re, b_im, c_re, c_im, ts):
    gp = S5_GROUPS // S5_PARTS
    dt = jnp.exp(log_dt)[:, None]
    ar, ai = lam_re * dt, lam_im * dt
    er = jnp.exp(ar)
    lbr, lbi = er * jnp.cos(ai), er * jnp.sin(ai)
    den = lam_re * lam_re + lam_im * lam_im
    qr = ((lbr - 1.0) * lam_re + lbi * lam_im) / den
    qi = (lbi * lam_re - (lbr - 1.0) * lam_im) / den
    bbr = qr[..., None] * b_re - qi[..., None] * b_im
    bbi = qr[..., None] * b_im + qi[..., None] * b_re
    eye = jnp.eye(gp, dtype=F32)

    def blockdiag_in(m):
        m = m.reshape(S5_PARTS, gp, S5_STATE, S5_GROUP)
        return jnp.einsum('qgph,gk->qghkp', m, eye).reshape(S5_PARTS, LANES, S5_PART_STATE)

    def blockdiag_out(m):
        m = m.reshape(S5_PARTS, gp, S5_GROUP, S5_STATE)
        return jnp.einsum('qghp,gk->qgpkh', m, eye).reshape(S5_PARTS, S5_PART_STATE, LANES)

    wb = jnp.concatenate([blockdiag_in(bbr), blockdiag_in(bbi)], axis=2).astype(BF16)
    wc = jnp.concatenate([blockdiag_out(c_re), -blockdiag_out(c_im)], axis=1).astype(BF16)

    def lanes(re, im):
        lead = re.shape[:-2]
        re = re.reshape(lead + (S5_PARTS, 1, S5_PART_STATE))
        im = im.reshape(lead + (S5_PARTS, 1, S5_PART_STATE))
        return jnp.concatenate([re, im], axis=-2).reshape(lead + (S5_PARTS * 2 * S5_PART_STATE,))

    j = jnp.arange(ts, dtype=F32)[:, None, None]

    def power(sign):
        mag = jnp.exp(sign * j * ar[None])
        ang = sign * j * ai[None]
        return lanes(mag * jnp.cos(ang), mag * jnp.sin(ang))

    return wb, power(-1.0), power(1.0), lanes(lbr, lbi)[None, :], wc


def _s5_mixer(p_main, bsz, seq, tables, d, glu_w, glu_b):
    ts = min(S5_TILE, seq)
    wb, pneg, ppos, lam1, wc = tables
    tri = jnp.tril(jnp.ones((CHUNK, CHUNK), F32)).astype(BF16)
    nsb = seq // ts
    consts = [wb, pneg, ppos, lam1, tri, wc, d.reshape(1, -1), glu_w.astype(BF16), glu_b.reshape(1, -1)]
    ucols = (COL_US5A // LANES, COL_US5A // LANES + 1, COL_US5B // LANES)
    return pl.pallas_call(
        _s5_kernel,
        grid=(bsz, nsb),
        in_specs=[pl.BlockSpec((ts, LANES), functools.partial(lambda b, s, col: (b * nsb + s, col), col=col))
                  for col in ucols] + [_full(a.shape) for a in consts],
        out_specs=pl.BlockSpec((ts, S5_WIDTH), lambda b, s: (b * nsb + s, 0)),
        out_shape=jax.ShapeDtypeStruct((bsz * seq, S5_WIDTH), BF16),
        scratch_shapes=[pltpu.VMEM((1, S5_PARTS * 2 * S5_PART_STATE), F32)],
        compiler_params=_params("parallel", "arbitrary"),
        name="s5_mixer",
    )(p_main, p_main, p_main, *consts)


INV_BASE = 16


def _block_inverse_rows(mt):
    groups, n, w = mt.shape
    mt2 = mt.reshape(groups * n, w)
    sub = lax.broadcasted_iota(jnp.int32, (groups * n, w), 0) % n
    lane = lax.broadcasted_iota(jnp.int32, (groups * n, w), 1)
    lane_in = lane % n
    lane_base = lane - lane_in
    x = (sub == lane_in).astype(F32)
    for i in range(1, n):
        coef = jnp.take_along_axis(mt2, lane_base + i, axis=1)
        acc = jnp.sum((coef * x).reshape(groups, n, w), axis=1, keepdims=True)
        new_row = jnp.where(lane_in[0:1, :] == i, 1.0, 0.0) - acc
        new_rows = jnp.broadcast_to(new_row, (groups, n, w)).reshape(groups * n, w)
        x = jnp.where(sub == i, new_rows, x)
    return x.reshape(groups, n, w)


def _unit_lower_inverse(a, at, ri, ci):
    n = a.shape[1]
    assert n == LANES
    diag_blk = ri // INV_BASE == ci // INV_BASE
    at_blk = jnp.where(diag_blk, at, 0.0)
    mt = at_blk[:, 0:INV_BASE, :]
    for blk in range(1, n // INV_BASE):
        mt = mt + at_blk[:, blk * INV_BASE:(blk + 1) * INV_BASE, :]
    x = _block_inverse_rows(mt)
    inv = jnp.where(diag_blk, jnp.concatenate([x] * (n // INV_BASE), axis=1), 0.0)
    size = INV_BASE
    while size < n:
        lower_left = (ri // (2 * size) == ci // (2 * size)) & ((ri // size) % 2 == 1) & ((ci // size) % 2 == 0)
        invb = inv.astype(BF16)
        inv = inv - _bmm(_bmm(invb, jnp.where(lower_left, a, 0.0).astype(BF16)).astype(BF16), invb)
        size *= 2
    return inv


def _gdn_kernel(qkv_ref, z_ref, small_ref, prm_ref, nw_ref, tri_ref, o_ref, state_ref):
    @pl.when(pl.program_id(1) == 0)
    def _():
        state_ref[...] = jnp.zeros_like(state_ref)

    ts = qkv_ref.shape[0]
    sm = small_ref[...]
    beta_all = _sigmoid(sm)
    g_all = -jnp.exp(prm_ref[0:1, :]) * _softplus(sm + prm_ref[1:2, :])
    gc = _select_rows(tri_ref[...], g_all)
    small_lane = lax.broadcasted_iota(jnp.int32, sm.shape, 1)
    rows_t = jnp.where(small_lane < SMALL_A, beta_all, gc).T
    nchunk = ts // CHUNK
    ri = lax.broadcasted_iota(jnp.int32, (CHUNK, CHUNK), 0)
    ci = lax.broadcasted_iota(jnp.int32, (CHUNK, CHUNK), 1)
    groups = [(c, hd) for c in range(nchunk) for hd in range(GDN_HEADS)]

    def head_tiles(off):
        return jnp.stack([qkv_ref[c * CHUNK:(c + 1) * CHUNK, off + hd * GDN_DK:off + (hd + 1) * GDN_DK]
                          for c, hd in groups]).astype(F32)

    def head_cols(arr, lane0):
        return jnp.stack([arr[c * CHUNK:(c + 1) * CHUNK, lane0 + hd:lane0 + hd + 1] for c, hd in groups])

    def head_rows(lane0):
        return jnp.stack([rows_t[lane0 + hd:lane0 + hd + 1, c * CHUNK:(c + 1) * CHUNK] for c, hd in groups])

    q = head_tiles(0)
    k = head_tiles(GDN_QK)
    v = head_tiles(2 * GDN_QK)
    q = q * lax.rsqrt(_lane_sum(q * q) + NORM_EPS) * (GDN_DK ** -0.5)
    k = k * lax.rsqrt(_lane_sum(k * k) + NORM_EPS)
    col, row = head_cols(gc, SMALL_A), head_rows(SMALL_A)
    beta, beta_row = head_cols(beta_all, SMALL_B), head_rows(SMALL_B)
    decay = jnp.exp(jnp.where(ri >= ci, col - row, -jnp.inf))
    decay_t = jnp.exp(jnp.where(ci >= ri, row - col, -jnp.inf))
    kb = k * beta
    kbf = k.astype(BF16)
    kk = _bmm_nt(kbf, kbf)
    a = jnp.where(ri > ci, kk * decay, 0.0) * beta
    at = jnp.where(ci > ri, kk * decay_t, 0.0) * beta_row
    inv = _unit_lower_inverse(a, at, ri, ci)
    eg = jnp.exp(col)
    rhs = jnp.concatenate([kb * eg, v * beta], axis=-1).astype(BF16)
    sol = _bmm(inv.astype(BF16), rhs)
    w, u = sol[:, :, :GDN_DK].astype(BF16), sol[:, :, GDN_DK:]
    attn = (_bmm_nt(q.astype(BF16), kbf) * decay).astype(BF16)
    g_last = col[:, CHUNK - 1:CHUNK, :]
    q_dec = (q * eg).astype(BF16)
    k_dec = (k * jnp.exp(g_last - col)).astype(BF16)
    e_last = jnp.exp(g_last)
    st = state_ref[...]
    for c in range(nchunk):
        sl = slice(c * GDN_HEADS, (c + 1) * GDN_HEADS)
        stb = st.astype(BF16)
        v_new = u[sl] - _bmm(w[sl], stb)
        vnb = v_new.astype(BF16)
        o = _bmm(q_dec[sl], stb) + _bmm(attn[sl], vnb)
        st = st * e_last[sl] + jnp.einsum('hik,hiv->hkv', k_dec[sl], vnb, preferred_element_type=F32)
        o = o * lax.rsqrt(_lane_sum(o * o) * (1.0 / GDN_DV) + NORM_EPS)
        for hd in range(GDN_HEADS):
            gate = z_ref[c * CHUNK:(c + 1) * CHUNK, hd * GDN_DV:(hd + 1) * GDN_DV].astype(F32)
            o_ref[c * CHUNK:(c + 1) * CHUNK, hd * GDN_DV:(hd + 1) * GDN_DV] = (
                o[hd] * nw_ref[...] * gate).astype(BF16)
    state_ref[...] = st


def _pad_lanes(v, start):
    return jnp.zeros((SMALL_COLS,), F32).at[start:start + v.shape[0]].set(v)


def _gdn_mixer(p_main, small, bsz, seq, a_log, dt_bias, norm_w):
    ts = min(GDN_TILE, seq)
    nsb = seq // ts
    width = 3 * GDN_QK
    prm = jnp.stack([_pad_lanes(a_log, SMALL_A), _pad_lanes(dt_bias, SMALL_A)])
    tri = jnp.kron(jnp.eye(ts // CHUNK, dtype=F32), jnp.tril(jnp.ones((CHUNK, CHUNK), F32))).astype(BF16)
    consts = [prm, norm_w.reshape(1, -1), tri]
    return pl.pallas_call(
        _gdn_kernel,
        grid=(bsz, nsb),
        in_specs=[
            pl.BlockSpec((ts, width), lambda b, s: (b * nsb + s, COL_QKV // width)),
            pl.BlockSpec((ts, GDN_WIDTH), lambda b, s: (b * nsb + s, COL_ZGDN // GDN_WIDTH)),
            pl.BlockSpec((ts, SMALL_COLS), lambda b, s: (b * nsb + s, 0)),
        ] + [_full(a.shape) for a in consts],
        out_specs=pl.BlockSpec((ts, GDN_WIDTH), lambda b, s: (b * nsb + s, 0)),
        out_shape=jax.ShapeDtypeStruct((bsz * seq, GDN_WIDTH), BF16),
        scratch_shapes=[pltpu.VMEM((GDN_HEADS, GDN_DK, GDN_DV), F32)],
        compiler_params=_params("parallel", "arbitrary"),
        name="gdn_mixer",
    )(p_main, p_main, small, *consts)


def _ssd_kernel(x_ref, z_ref, bc_ref, small_ref, prm_ref, hv_ref, tri_ref, expand_ref, smask_ref, o_ref,
                state_ref):
    @pl.when(pl.program_id(1) == 0)
    def _():
        state_ref[...] = jnp.zeros_like(state_ref)

    x = x_ref[...].astype(F32)
    ts = x.shape[0]
    bmb, cmb = bc_ref[:, :SSD_BC], bc_ref[:, SSD_BC:]
    cm = cmb.astype(F32)
    sm = small_ref[...]
    dt_all = _softplus(sm + prm_ref[1:2, :])
    la_all = dt_all * (-jnp.exp(prm_ref[0:1, :]))
    cs = _select_rows(tri_ref[...], la_all)
    cst = cs.T
    expand = expand_ref[...]
    dtx = _select_cols(dt_all, expand)
    csx = _select_cols(cs, expand)
    cs_last_x = csx[ts - 1:ts, :]
    xdt = x * dtx
    xdtb = xdt.astype(BF16)
    ri = lax.broadcasted_iota(jnp.int32, (ts, ts), 0)
    ci = lax.broadcasted_iota(jnp.int32, (ts, ts), 1)
    lane = lax.broadcasted_iota(jnp.int32, (ts, SSD_BC), 1)
    cb = []
    for g in range(SSD_GROUPS):
        in_g = (lane >= g * SSD_STATE) & (lane < (g + 1) * SSD_STATE)
        cb.append(_dot_nt(jnp.where(in_g, cm, 0.0).astype(BF16), bmb))
    halves = []
    for pair in range(SSD_HEADS // 2):
        rhs = xdtb[:, pair * LANES:(pair + 1) * LANES]
        ys = []
        for hd in (2 * pair, 2 * pair + 1):
            col = cs[:, SMALL_DT + hd:SMALL_DT + hd + 1]
            row = cst[SMALL_DT + hd:SMALL_DT + hd + 1, :]
            decay = jnp.exp(jnp.where(ri >= ci, col - row, -jnp.inf))
            ys.append(_dot((cb[hd // SSD_HPG] * decay).astype(BF16), rhs))
        halves.append(jnp.where(lane < SSD_HEAD_DIM, ys[0], ys[1]))
    y = jnp.concatenate(halves, axis=-1)
    st = state_ref[...]
    y = y + _dot(cmb, st.astype(BF16)) * jnp.exp(csx)
    to_end = jnp.exp(cs_last_x - csx)
    upd = _dot_tn(bmb, (xdt * to_end).astype(BF16))
    state_ref[...] = st * jnp.exp(cs_last_x) + upd * smask_ref[...]
    y = y + hv_ref[0:1, :] * x
    y = y * z_ref[...].astype(F32)
    y = y * lax.rsqrt(jnp.mean(y * y, axis=-1, keepdims=True) + NORM_EPS)
    o_ref[...] = (y * hv_ref[1:2, :]).astype(BF16)


def _ssd_mixer(p_main, small, bsz, seq, a_log, dt_bias, d, norm_w):
    ts = min(CHUNK, seq)
    nsb = seq // ts
    prm = jnp.stack([_pad_lanes(a_log, SMALL_DT), _pad_lanes(dt_bias, SMALL_DT)])
    hv = jnp.stack([jnp.repeat(d, SSD_HEAD_DIM), norm_w])
    tri = jnp.tril(jnp.ones((ts, ts), F32)).astype(BF16)
    lane_head = jnp.arange(SSD_WIDTH) // SSD_HEAD_DIM
    expand = (jnp.arange(SMALL_COLS)[:, None] == SMALL_DT + lane_head[None, :]).astype(BF16)
    row_group = jnp.arange(SSD_BC) // SSD_STATE
    smask = (row_group[:, None] == (lane_head // SSD_HPG)[None, :]).astype(F32)
    consts = [prm, hv, tri, expand, smask]
    bc_w = 2 * SSD_BC
    return pl.pallas_call(
        _ssd_kernel,
        grid=(bsz, nsb),
        in_specs=[
            pl.BlockSpec((ts, SSD_WIDTH), lambda b, s: (b * nsb + s, COL_XSSD // SSD_WIDTH)),
            pl.BlockSpec((ts, SSD_WIDTH), lambda b, s: (b * nsb + s, COL_ZSSD // SSD_WIDTH)),
            pl.BlockSpec((ts, bc_w), lambda b, s: (b * nsb + s, COL_BC // bc_w)),
            pl.BlockSpec((ts, SMALL_COLS), lambda b, s: (b * nsb + s, 0)),
        ] + [_full(a.shape) for a in consts],
        out_specs=pl.BlockSpec((ts, SSD_WIDTH), lambda b, s: (b * nsb + s, 0)),
        out_shape=jax.ShapeDtypeStruct((bsz * seq, SSD_WIDTH), BF16),
        scratch_shapes=[pltpu.VMEM((SSD_BC, SSD_WIDTH), F32)],
        compiler_params=_params("parallel", "arbitrary"),
        name="ssd_mixer",
    )(p_main, p_main, p_main, small, *consts)


def _lru_kernel(x_ref, g_ref, wr_ref, wi_ref, vec_ref, o_ref, h_ref):
    step = pl.program_id(1)

    @pl.when(step == 0)
    def _():
        h_ref[...] = jnp.zeros_like(h_ref)

    xb = x_ref[...]
    x = xb.astype(F32)
    ts = x.shape[0]
    r = _sigmoid(_dot(xb, wr_ref[...]) + vec_ref[0:1, :])
    gate_i = _sigmoid(_dot(xb, wi_ref[...]) + vec_ref[1:2, :])
    log_a = -LRU_C * r * _softplus(-vec_ref[2:3, :])
    a = jnp.exp(log_a)
    one_minus_a2 = -jnp.tanh(log_a) * (a * a + 1.0)
    mult = one_minus_a2 * lax.rsqrt(jnp.maximum(one_minus_a2, 1e-30))
    row = lax.broadcasted_iota(jnp.int32, (ts, LRU_WIDTH), 0)
    mult = jnp.where(row + step * ts == 0, 1.0, mult)
    b = mult * gate_i * x

    row_in = row % SUBLANES
    shift = 1
    while shift < SUBLANES:
        keep = row_in >= shift
        a_prev = jnp.where(keep, pltpu.roll(a, shift, 0), 1.0)
        b_prev = jnp.where(keep, pltpu.roll(b, shift, 0), 0.0)
        b = a * b_prev + b
        a = a * a_prev
        shift *= 2
    carry = h_ref[...]
    blocks = []
    for blk in range(ts // SUBLANES):
        rows = slice(blk * SUBLANES, (blk + 1) * SUBLANES)
        blocks.append(b[rows, :] + a[rows, :] * carry)
        carry = blocks[-1][SUBLANES - 1:SUBLANES, :]
    h_ref[...] = carry
    o_ref[...] = (jnp.concatenate(blocks, axis=0) * g_ref[...].astype(F32)).astype(BF16)


def _lru_mixer(p_main, bsz, seq, lam, wr, br, wi, bi):
    ts = min(LRU_TILE, seq)
    nsb = seq // ts
    eye = jnp.eye(LRU_BLOCKS, dtype=F32)

    def blockdiag(w):
        return jnp.einsum('nde,nm->ndme', w, eye).reshape(LRU_WIDTH, LRU_WIDTH).astype(BF16)

    consts = [blockdiag(wr), blockdiag(wi), jnp.stack([br, bi, lam])]
    return pl.pallas_call(
        _lru_kernel,
        grid=(bsz, nsb),
        in_specs=[
            pl.BlockSpec((ts, LRU_WIDTH), lambda b, s: (b * nsb + s, COL_XLRU // LRU_WIDTH)),
            pl.BlockSpec((ts, LRU_WIDTH), lambda b, s: (b * nsb + s, COL_GLRU // LRU_WIDTH)),
        ] + [_full(a.shape) for a in consts],
        out_specs=pl.BlockSpec((ts, LRU_WIDTH), lambda b, s: (b * nsb + s, 0)),
        out_shape=jax.ShapeDtypeStruct((bsz * seq, LRU_WIDTH), BF16),
        scratch_shapes=[pltpu.VMEM((1, LRU_WIDTH), F32)],
        compiler_params=_params("parallel", "arbitrary"),
        name="lru_mixer",
    )(p_main, p_main, *consts)


def _merge_kernel(ya_ref, yb_ref, yc_ref, yd_ref, gm_ref, x_ref, mod_ref, wa_ref, wb_ref, wc_ref, wd_ref,
                  wo_ref, o_ref):
    merged = None
    for i, (y_ref, w_ref) in enumerate(((ya_ref, wa_ref), (yb_ref, wb_ref), (yc_ref, wc_ref), (yd_ref, wd_ref))):
        gate = gm_ref[:, i * D_MODEL:(i + 1) * D_MODEL].astype(F32)
        term = gate * _dot(y_ref[...], w_ref[...])
        merged = term if merged is None else merged + term
    out = _dot(merged.astype(BF16), wo_ref[...])
    o_ref[...] = x_ref[...] + mod_ref[:, 2 * D_MODEL:3 * D_MODEL] * out


def _merge(ys, p_main, x2, mod3, w_branch, w_out, seq):
    t = x2.shape[0]
    tm = min(512, seq)
    per_b = seq // tm
    widths = (S5_WIDTH, GDN_WIDTH, SSD_WIDTH, LRU_WIDTH)
    rows, acc = [], 0
    for wd in widths:
        rows.append(w_branch[acc:acc + wd].astype(BF16))
        acc += wd
    gw = N_BRANCH * D_MODEL
    return pl.pallas_call(
        _merge_kernel,
        grid=(t // tm,),
        in_specs=[pl.BlockSpec((tm, wd), lambda i: (i, 0)) for wd in widths] + [
            pl.BlockSpec((tm, gw), lambda i: (i, COL_GMERGE // gw)),
            pl.BlockSpec((tm, D_MODEL), lambda i: (i, 0)),
            pl.BlockSpec((None, 1, 6 * D_MODEL), lambda i: (i // per_b, 0, 0)),
        ] + [_full(r.shape) for r in rows] + [_full((D_MODEL, D_MODEL))],
        out_specs=pl.BlockSpec((tm, D_MODEL), lambda i: (i, 0)),
        out_shape=jax.ShapeDtypeStruct((t, D_MODEL), F32),
        compiler_params=_params("parallel"),
        name="merge_out",
    )(*ys, p_main, x2, mod3, *rows, w_out.astype(BF16))


FFN_SLICES = tuple((s, min(512, FFN_HIDDEN - s)) for s in range(0, FFN_HIDDEN, 512))


def _ffn_kernel(x_ref, mod_ref, g_ref, w13_ref, w2_ref, gf_ref, o_ref, *, final):
    x = x_ref[...]
    h = _modulated_norm(x, g_ref[...], mod_ref[:, 3 * D_MODEL:4 * D_MODEL], mod_ref[:, 4 * D_MODEL:5 * D_MODEL])
    hb = h.astype(BF16)
    acc = None
    for start, size in FFN_SLICES:
        a = _dot(hb, w13_ref[:, start:start + size])
        b = _dot(hb, w13_ref[:, FFN_HIDDEN + start:FFN_HIDDEN + start + size])
        part = _dot((_silu(a) * b).astype(BF16), w2_ref[start:start + size, :])
        acc = part if acc is None else acc + part
    x = x + mod_ref[:, 5 * D_MODEL:6 * D_MODEL] * acc
    if final:
        x = x * lax.rsqrt(jnp.mean(x * x, axis=-1, keepdims=True) + NORM_EPS) * gf_ref[...]
    o_ref[...] = x


def _ffn(x2, mod3, ln_g, w13, w2, ln_final, seq, final):
    t = x2.shape[0]
    tm = min(512, seq)
    per_b = seq // tm
    return pl.pallas_call(
        functools.partial(_ffn_kernel, final=final),
        grid=(t // tm,),
        in_specs=[
            pl.BlockSpec((tm, D_MODEL), lambda i: (i, 0)),
            pl.BlockSpec((None, 1, 6 * D_MODEL), lambda i: (i // per_b, 0, 0)),
            _full((1, D_MODEL)),
            pl.BlockSpec((D_MODEL, 2 * FFN_HIDDEN), lambda i: (0, 0), pipeline_mode=pl.Buffered(1)),
            pl.BlockSpec((FFN_HIDDEN, D_MODEL), lambda i: (0, 0), pipeline_mode=pl.Buffered(1)),
            _full((1, D_MODEL)),
        ],
        out_specs=pl.BlockSpec((tm, D_MODEL), lambda i: (i, 0)),
        out_shape=jax.ShapeDtypeStruct((t, D_MODEL), F32),
        compiler_params=_params("parallel"),
        name="ffn_final" if final else "ffn",
    )(x2, mod3, ln_g, w13, w2, ln_final)


def _split_in_weights(w_in, conv_w, conv_b):
    o = 0
    seg = {}
    for name, width in (("q", 512), ("k", 512), ("v", 512), ("xssd", 512), ("bssd", 128), ("cssd", 128),
                        ("xlru", 512), ("us5", 384), ("bgdn", 4), ("agdn", 4), ("zgdn", 512), ("zssd", 512),
                        ("dtssd", 8), ("glru", 512), ("gmerge", 4096)):
        seg[name] = (o, o + width)
        o += width

    def cols(arr, names):
        return jnp.concatenate([arr[:, seg[n][0]:seg[n][1]] for n in names], axis=1)

    conv_names = ("q", "k", "v", "xssd", "xlru", "bssd", "cssd")
    us5 = w_in[:, seg["us5"][0]:seg["us5"][1]]
    main = jnp.concatenate([cols(w_in, conv_names), us5[:, :2 * LANES], cols(w_in, ("zgdn", "zssd", "gmerge", "glru")),
                            us5[:, 2 * LANES:]], axis=1).astype(BF16)
    pad = jnp.zeros((w_in.shape[0], SMALL_COLS - 16), w_in.dtype)
    small = jnp.concatenate([cols(w_in, ("bgdn", "agdn", "dtssd")), pad], axis=1).astype(BF16)
    return main, small, cols(conv_w, conv_names), cols(conv_b.reshape(1, -1), conv_names)


def kernel(x, c, ln_mix_g, ln_ffn_g, ln_final_g, ada_w, ada_b, w_in, conv_w, conv_b, s5_lambda_re, s5_lambda_im, s5_log_dt, s5_b_re, s5_b_im, s5_c_re, s5_c_im, s5_d, s5_glu_w, s5_glu_b, gdn_a_log, gdn_dt_bias, gdn_norm_w, ssd_a_log, ssd_dt_bias, ssd_d, ssd_norm_w, lru_lambda, lru_wr, lru_br, lru_wi, lru_bi, w_branch, w_out, ffn_w13, ffn_w2):
    bsz, seq, d = x.shape
    t = bsz * seq
    x2 = x.reshape(t, d)
    mod = _ada_mod(c, ada_w, ada_b)
    ln_final = ln_final_g.reshape(1, d)
    for l in range(DEPTH):
        mod3 = mod[l].reshape(bsz, 1, 6 * d)
        w_main, w_small, cw, cb = _split_in_weights(w_in[l], conv_w[l], conv_b[l])
        p_main, small = _inproj(x2, mod3, ln_mix_g[l].reshape(1, d), w_main, w_small, cw, cb, seq)

        tables = _s5_tables(s5_lambda_re[l], s5_lambda_im[l], s5_log_dt[l], s5_b_re[l], s5_b_im[l],
                            s5_c_re[l], s5_c_im[l], min(CHUNK, seq))
        y_a = _s5_mixer(p_main, bsz, seq, tables, s5_d[l], s5_glu_w[l], s5_glu_b[l])
        y_b = _gdn_mixer(p_main, small, bsz, seq, gdn_a_log[l], gdn_dt_bias[l], gdn_norm_w[l])
        y_c = _ssd_mixer(p_main, small, bsz, seq, ssd_a_log[l], ssd_dt_bias[l], ssd_d[l], ssd_norm_w[l])
        y_d = _lru_mixer(p_main, bsz, seq, lru_lambda[l], lru_wr[l], lru_br[l], lru_wi[l], lru_bi[l])
        x2 = _merge((y_a, y_b, y_c, y_d), p_main, x2, mod3, w_branch[l], w_out[l], seq)
        x2 = _ffn(x2, mod3, ln_ffn_g[l].reshape(1, d), ffn_w13[l].astype(BF16), ffn_w2[l].astype(BF16),
                  ln_final, seq, final=(l == DEPTH - 1))
    return x2.reshape(bsz, seq, d)
```

```python
import functools

import jax
import jax.numpy as jnp
from jax import lax
from jax.experimental import pallas as pl
from jax.experimental.pallas import tpu as pltpu

F32 = jnp.float32
BF16 = jnp.bfloat16

D_MODEL = 1024
DEPTH = 2
CONV_K = 4
NORM_EPS = 1e-6
S5_GROUP, S5_STATE, S5_WIDTH, S5_GROUPS = 16, 64, 384, 24
GDN_HEADS, GDN_DK, GDN_DV = 4, 128, 128
GDN_QK = GDN_HEADS * GDN_DK
GDN_WIDTH = GDN_HEADS * GDN_DV
SSD_HEAD_DIM, SSD_WIDTH, SSD_HEADS, SSD_GROUPS, SSD_HPG, SSD_STATE = 64, 512, 8, 2, 4, 64
SSD_BC = SSD_GROUPS * SSD_STATE
LRU_WIDTH, LRU_BLOCK, LRU_BLOCKS, LRU_C = 512, 64, 8, 8.0
N_BRANCH = 4
FFN_HIDDEN = 2816

LANES = 128
SUBLANES = 8

COL_QKV, COL_XSSD, COL_XLRU, COL_BC = 0, 1536, 2048, 2560
CONV_COLS = 2816
SILU_CONV_END = 2048
COL_US5A, COL_ZGDN, COL_ZSSD, COL_GMERGE, COL_GLRU, COL_US5B = 2816, 3072, 3584, 4096, 8192, 8704
MAIN_COLS = 8832
PROJ_TILES = 3
PROJ_TN = MAIN_COLS // PROJ_TILES
PROJ_ROW_CHUNKS = 2
SMALL_COLS = 128
SMALL_B, SMALL_A, SMALL_DT = 0, 4, 8

CHUNK = 128
MIX_TILE = 256
S5_PARTS = S5_WIDTH // LANES
S5_PART_STATE = (S5_GROUPS // S5_PARTS) * S5_STATE
VMEM_LIMIT = 56 * 1024 * 1024


def _sigmoid(x):
    return 0.5 * jnp.tanh(0.5 * x) + 0.5


def _silu(x):
    return x * _sigmoid(x)


def _softplus(x):
    return jnp.maximum(x, 0.0) + jnp.log1p(jnp.exp(-jnp.abs(x)))


def _gelu(x):
    return jax.nn.gelu(x, approximate=True)


def _dot(a, b):
    return jnp.dot(a, b, preferred_element_type=F32)


def _dot_nt(a, b):
    return lax.dot_general(a, b, (((1,), (1,)), ((), ())), preferred_element_type=F32)


def _dot_tn(a, b):
    return lax.dot_general(a, b, (((0,), (0,)), ((), ())), preferred_element_type=F32)


def _bmm(a, b):
    return jnp.einsum('gij,gjk->gik', a, b, preferred_element_type=F32)


def _bmm_nt(a, b):
    return jnp.einsum('gik,gjk->gij', a, b, preferred_element_type=F32)


def _split3(x):
    x1 = x.astype(BF16)
    r1 = x - x1.astype(F32)
    x2 = r1.astype(BF16)
    x3 = (r1 - x2.astype(F32)).astype(BF16)
    return x1, x2, x3


def _select_rows(sel_bf16, x):
    x1, x2, x3 = _split3(x)
    return _dot(sel_bf16, x1) + (_dot(sel_bf16, x2) + _dot(sel_bf16, x3))


def _select_cols(x, sel_bf16):
    x1, x2, x3 = _split3(x)
    return _dot(x1, sel_bf16) + (_dot(x2, sel_bf16) + _dot(x3, sel_bf16))


def _lane_sum(x):
    shape = x.shape
    ones = jnp.ones((shape[-1], shape[-1]), BF16)
    return _dot(x.reshape(-1, shape[-1]).astype(BF16), ones).reshape(shape)


def _params(*sem):
    return pltpu.CompilerParams(dimension_semantics=sem, vmem_limit_bytes=VMEM_LIMIT)


def _full(shape):
    n = len(shape)
    return pl.BlockSpec(shape, lambda *_: (0,) * n)


def _ada_kernel(c_ref, w_ref, b_ref, o_ref):
    cond = _silu(c_ref[...]).astype(BF16)
    o_ref[...] = _dot(cond, w_ref[...].astype(BF16)) + b_ref[...]


def _ada_mod(c, ada_w, ada_b):
    depth, d, n = ada_w.shape
    bsz = c.shape[0]
    tn = 1536
    return pl.pallas_call(
        _ada_kernel,
        grid=(depth, n // tn),
        in_specs=[
            pl.BlockSpec((bsz, d), lambda l, j: (0, 0)),
            pl.BlockSpec((None, d, tn), lambda l, j: (l, 0, j)),
            pl.BlockSpec((None, 1, tn), lambda l, j: (l, 0, j)),
        ],
        out_specs=pl.BlockSpec((None, bsz, tn), lambda l, j: (l, 0, j)),
        out_shape=jax.ShapeDtypeStruct((depth, bsz, n), F32),
        compiler_params=_params("parallel", "parallel"),
        name="ada_mod",
    )(c, ada_w, ada_b.reshape(depth, 1, n))


def _modulated_norm(x, g, shift, scale):
    ms = jnp.mean(x * x, axis=-1, keepdims=True)
    return x * lax.rsqrt(ms + NORM_EPS) * g * (1.0 + scale) + shift


CONV_STRIDE = 4


def _causal_conv(x, xpad_ref, stage_ref, w_ref, b_ref, emit):
    ts, c = x.shape
    span = CONV_STRIDE * SUBLANES

    for tile in range(c // LANES):
        lanes = slice(tile * LANES, (tile + 1) * LANES)
        xt = x[:, lanes]
        xpad_ref[tile, SUBLANES:SUBLANES + ts, :] = xt
        taps = [jnp.broadcast_to(w_ref[k:k + 1, lanes], (SUBLANES, LANES)) for k in range(CONV_K)]
        bias = jnp.broadcast_to(b_ref[:, lanes], (SUBLANES, LANES))
        for group in range(ts // span):
            for res in range(CONV_STRIDE):
                row0 = group * span + res
                acc = bias
                for k in range(CONV_K):
                    start = SUBLANES + row0 - (CONV_K - 1) + k
                    acc = acc + taps[k] * xpad_ref[tile, pl.ds(start, SUBLANES, stride=CONV_STRIDE), :]
                stage_ref[tile, pl.ds(row0, SUBLANES, stride=CONV_STRIDE), :] = acc
        xpad_ref[tile, 0:SUBLANES, :] = xt[ts - SUBLANES:, :]
        emit(tile, stage_ref[tile])


def _inproj_kernel(x_ref, mod_ref, g_ref, w_ref, ws_ref, cw_ref, cb_ref, p_ref, small_ref, xpad_ref, stage_ref, *,
                   per_b):
    rows = x_ref.shape[0] // PROJ_ROW_CHUNKS

    @pl.when(pl.program_id(0) % per_b == 0)
    def _():
        xpad_ref[:, 0:SUBLANES, :] = jnp.zeros((CONV_COLS // LANES, SUBLANES, LANES), F32)

    def column_tile_0(r, put):
        def emit(tile, y):
            lo = tile * LANES
            plain = SILU_CONV_END <= lo < COL_BC
            put(lo, lo + LANES, y if plain else _silu(y))

        _causal_conv(r[:, :CONV_COLS], xpad_ref, stage_ref, cw_ref, cb_ref, emit)
        put(CONV_COLS, PROJ_TN, r[:, CONV_COLS:])

    def column_tile_1(r, put):
        z0, g0 = COL_ZGDN - PROJ_TN, COL_GMERGE - PROJ_TN
        put(PROJ_TN, PROJ_TN + z0, r[:, :z0])
        put(PROJ_TN + z0, PROJ_TN + g0, _silu(r[:, z0:g0]))
        put(PROJ_TN + g0, 2 * PROJ_TN, _sigmoid(r[:, g0:]))

    def column_tile_2(r, put):
        l0, u0 = COL_GLRU - 2 * PROJ_TN, COL_US5B - 2 * PROJ_TN
        put(2 * PROJ_TN, 2 * PROJ_TN + l0, _sigmoid(r[:, :l0]))
        put(2 * PROJ_TN + l0, 2 * PROJ_TN + u0, _gelu(r[:, l0:u0]))
        put(2 * PROJ_TN + u0, 3 * PROJ_TN, r[:, u0:])

    for rc in range(PROJ_ROW_CHUNKS):
        r0 = rc * rows
        h = _modulated_norm(x_ref[r0:r0 + rows, :], g_ref[...], mod_ref[:, 0:D_MODEL],
                            mod_ref[:, D_MODEL:2 * D_MODEL])
        hb = h.astype(BF16)
        small_ref[r0:r0 + rows, :] = _dot(hb, ws_ref[...])

        def put(lo, hi, val, r0=r0):
            p_ref[r0:r0 + rows, lo:hi] = val.astype(BF16)

        for jt, epilogue in enumerate((column_tile_0, column_tile_1, column_tile_2)):
            epilogue(_dot(hb, w_ref[:, jt * PROJ_TN:(jt + 1) * PROJ_TN]), put)


def _inproj(x2, mod3, ln_g, w_main, w_small, conv_w, conv_b, seq):
    t = x2.shape[0]
    tm = min(512, seq)
    per_b = seq // tm
    return pl.pallas_call(
        functools.partial(_inproj_kernel, per_b=per_b),
        grid=(t // tm,),
        in_specs=[
            pl.BlockSpec((tm, D_MODEL), lambda i: (i, 0)),
            pl.BlockSpec((None, 1, 6 * D_MODEL), lambda i: (i // per_b, 0, 0)),
            pl.BlockSpec((1, D_MODEL), lambda i: (0, 0)),
            pl.BlockSpec((D_MODEL, MAIN_COLS), lambda i: (0, 0), pipeline_mode=pl.Buffered(1)),
            pl.BlockSpec((D_MODEL, SMALL_COLS), lambda i: (0, 0)),
            pl.BlockSpec((CONV_K, CONV_COLS), lambda i: (0, 0)),
            pl.BlockSpec((1, CONV_COLS), lambda i: (0, 0)),
        ],
        out_specs=[
            pl.BlockSpec((tm, MAIN_COLS), lambda i: (i, 0)),
            pl.BlockSpec((tm, SMALL_COLS), lambda i: (i, 0)),
        ],
        out_shape=[
            jax.ShapeDtypeStruct((t, MAIN_COLS), BF16),
            jax.ShapeDtypeStruct((t, SMALL_COLS), F32),
        ],
        scratch_shapes=[pltpu.VMEM((CONV_COLS // LANES, tm // PROJ_ROW_CHUNKS + SUBLANES, LANES), F32),
                        pltpu.VMEM((CONV_COLS // LANES, tm // PROJ_ROW_CHUNKS, LANES), F32)],
        compiler_params=_params("arbitrary"),
        name="inproj",
    )(x2, mod3, ln_g, w_main, w_small, conv_w, conv_b)


def _s5_kernel(u0_ref, u1_ref, u2_ref, wb_ref, pneg_ref, ppos_ref, lam1_ref, tri_ref, wc_ref, d_ref, gw_ref,
               gb_ref, o_ref, carry_ref):
    h = S5_PART_STATE
    nchunk = u0_ref.shape[0] // CHUNK
    tri = tri_ref[...]
    ys = [[None] * S5_PARTS for _ in range(nchunk)]
    for part, u_ref in enumerate((u0_ref, u1_ref, u2_ref)):
        lo = part * 2 * h
        ub = u_ref[...]
        bu = _dot(ub, wb_ref[part])
        nr, ni = pneg_ref[:, lo:lo + h], pneg_ref[:, lo + h:lo + 2 * h]
        sums = []
        for c in range(nchunk):
            br, bi = bu[c * CHUNK:(c + 1) * CHUNK, :h], bu[c * CHUNK:(c + 1) * CHUNK, h:]
            z = jnp.concatenate([nr * br - ni * bi, nr * bi + ni * br], axis=-1).astype(BF16)
            sums.append(_dot(tri, z))
        cr, ci = carry_ref[:, lo:lo + h], carry_ref[:, lo + h:lo + 2 * h]
        l1r, l1i = lam1_ref[:, lo:lo + h], lam1_ref[:, lo + h:lo + 2 * h]
        pr, pi = ppos_ref[:, lo:lo + h], ppos_ref[:, lo + h:lo + 2 * h]
        for c in range(nchunk):
            s = sums[c]
            sr = s[:, :h] + (l1r * cr - l1i * ci)
            si = s[:, h:] + (l1r * ci + l1i * cr)
            xr = pr * sr - pi * si
            xi = pr * si + pi * sr
            cr, ci = xr[CHUNK - 1:CHUNK, :], xi[CHUNK - 1:CHUNK, :]
            xcat = jnp.concatenate([xr, xi], axis=-1).astype(BF16)
            skip = d_ref[:, part * LANES:(part + 1) * LANES] * ub[c * CHUNK:(c + 1) * CHUNK, :].astype(F32)
            ys[c][part] = _dot(xcat, wc_ref[part]) + skip
        carry_ref[:, lo:lo + h] = cr
        carry_ref[:, lo + h:lo + 2 * h] = ci
    y = _gelu(jnp.concatenate([jnp.concatenate(row, axis=-1) for row in ys], axis=0))
    o_ref[...] = (y * _sigmoid(_dot(y.astype(BF16), gw_ref[...]) + gb_ref[...])).astype(BF16)


def _s5_tables(lam_re, lam_im, log_dt, b_re, b_im, c_re, c_im, ts):
    gp = S5_GROUPS // S5_PARTS
    dt = jnp.exp(log_dt)[:, None]
    ar, ai = lam_re * dt, lam_im * dt
    er = jnp.exp(ar)
    lbr, lbi = er * jnp.cos(ai), er * jnp.sin(ai)
    den = lam_re * lam_re + lam_im * lam_im
    qr = ((lbr - 1.0) * lam_re + lbi * lam_im) / den
    qi = (lbi * lam_re - (lbr - 1.0) * lam_im) / den
    bbr = qr[..., None] * b_re - qi[..., None] * b_im
    bbi = qr[..., None] * b_im + qi[..., None] * b_re
    eye = jnp.eye(gp, dtype=F32)

    def blockdiag_in(m):
        m = m.reshape(S5_PARTS, gp, S5_STATE, S5_GROUP)
        return jnp.einsum('qgph,gk->qghkp', m, eye).reshape(S5_PARTS, LANES, S5_PART_STATE)

    def blockdiag_out(m):
        m = m.reshape(S5_PARTS, gp, S5_GROUP, S5_STATE)
        return jnp.einsum('qghp,gk->qgpkh', m, eye).reshape(S5_PARTS, S5_PART_STATE, LANES)

    wb = jnp.concatenate([blockdiag_in(bbr), blockdiag_in(bbi)], axis=2).astype(BF16)
    wc = jnp.concatenate([blockdiag_out(c_re), -blockdiag_out(c_im)], axis=1).astype(BF16)

    def lanes(re, im):
        lead = re.shape[:-2]
        re = re.reshape(lead + (S5_PARTS, 1, S5_PART_STATE))
        im = im.reshape(lead + (S5_PARTS, 1, S5_PART_STATE))
        return jnp.concatenate([re, im], axis=-2).reshape(lead + (S5_PARTS * 2 * S5_PART_STATE,))

    j = jnp.arange(ts, dtype=F32)[:, None, None]

    def power(sign):
        mag = jnp.exp(sign * j * ar[None])
        ang = sign * j * ai[None]
        return lanes(mag * jnp.cos(ang), mag * jnp.sin(ang))

    return wb, power(-1.0), power(1.0), lanes(lbr, lbi)[None, :], wc


def _s5_mixer(tables, d, glu_w, glu_b):
    wb, pneg, ppos, lam1, wc = tables
    tri = jnp.tril(jnp.ones((CHUNK, CHUNK), F32)).astype(BF16)
    consts = [wb, pneg, ppos, lam1, tri, wc, d.reshape(1, -1), glu_w.astype(BF16), glu_b.reshape(1, -1)]
    cols = [(COL_US5A, LANES), (COL_US5A + LANES, LANES), (COL_US5B, LANES)]
    return dict(body=_s5_kernel, cols=cols, small=False, consts=consts, width=S5_WIDTH,
                scratch=pltpu.VMEM((1, S5_PARTS * 2 * S5_PART_STATE), F32))


INV_BASE = 16


def _block_inverse_rows(mt):
    groups, n, w = mt.shape
    mt2 = mt.reshape(groups * n, w)
    sub = lax.broadcasted_iota(jnp.int32, (groups * n, w), 0) % n
    lane = lax.broadcasted_iota(jnp.int32, (groups * n, w), 1)
    lane_in = lane % n
    lane_base = lane - lane_in
    x = (sub == lane_in).astype(F32)
    for i in range(1, n):
        coef = jnp.take_along_axis(mt2, lane_base + i, axis=1)
        acc = jnp.sum((coef * x).reshape(groups, n, w), axis=1, keepdims=True)
        new_row = jnp.where(lane_in[0:1, :] == i, 1.0, 0.0) - acc
        new_rows = jnp.broadcast_to(new_row, (groups, n, w)).reshape(groups * n, w)
        x = jnp.where(sub == i, new_rows, x)
    return x.reshape(groups, n, w)


def _unit_lower_inverse(a, at, ri, ci):
    n = a.shape[1]
    assert n == LANES
    diag_blk = ri // INV_BASE == ci // INV_BASE
    at_blk = jnp.where(diag_blk, at, 0.0)
    mt = at_blk[:, 0:INV_BASE, :]
    for blk in range(1, n // INV_BASE):
        mt = mt + at_blk[:, blk * INV_BASE:(blk + 1) * INV_BASE, :]
    x = _block_inverse_rows(mt)
    inv = jnp.where(diag_blk, jnp.concatenate([x] * (n // INV_BASE), axis=1), 0.0)
    size = INV_BASE
    while size < n:
        lower_left = (ri // (2 * size) == ci // (2 * size)) & ((ri // size) % 2 == 1) & ((ci // size) % 2 == 0)
        invb = inv.astype(BF16)
        inv = inv - _bmm(_bmm(invb, jnp.where(lower_left, a, 0.0).astype(BF16)).astype(BF16), invb)
        size *= 2
    return inv


def _gdn_kernel(qkv_ref, z_ref, small_ref, prm_ref, nw_ref, tri_ref, o_ref, state_ref):
    ts = qkv_ref.shape[0]
    sm = small_ref[...]
    beta_all = _sigmoid(sm)
    g_all = -jnp.exp(prm_ref[0:1, :]) * _softplus(sm + prm_ref[1:2, :])
    gc = _select_rows(tri_ref[...], g_all)
    small_lane = lax.broadcasted_iota(jnp.int32, sm.shape, 1)
    rows_t = jnp.where(small_lane < SMALL_A, beta_all, gc).T
    nchunk = ts // CHUNK
    ri = lax.broadcasted_iota(jnp.int32, (CHUNK, CHUNK), 0)
    ci = lax.broadcasted_iota(jnp.int32, (CHUNK, CHUNK), 1)
    groups = [(c, hd) for c in range(nchunk) for hd in range(GDN_HEADS)]

    def head_tiles(off):
        return jnp.stack([qkv_ref[c * CHUNK:(c + 1) * CHUNK, off + hd * GDN_DK:off + (hd + 1) * GDN_DK]
                          for c, hd in groups]).astype(F32)

    def head_cols(arr, lane0):
        return jnp.stack([arr[c * CHUNK:(c + 1) * CHUNK, lane0 + hd:lane0 + hd + 1] for c, hd in groups])

    def head_rows(lane0):
        return jnp.stack([rows_t[lane0 + hd:lane0 + hd + 1, c * CHUNK:(c + 1) * CHUNK] for c, hd in groups])

    q = head_tiles(0)
    k = head_tiles(GDN_QK)
    v = head_tiles(2 * GDN_QK)
    q = q * lax.rsqrt(_lane_sum(q * q) + NORM_EPS) * (GDN_DK ** -0.5)
    k = k * lax.rsqrt(_lane_sum(k * k) + NORM_EPS)
    wide = (len(groups), CHUNK, LANES)
    col, row = jnp.broadcast_to(head_cols(gc, SMALL_A), wide), head_rows(SMALL_A)
    beta, beta_row = jnp.broadcast_to(head_cols(beta_all, SMALL_B), wide), head_rows(SMALL_B)
    decay = jnp.exp(jnp.where(ri >= ci, col - row, -jnp.inf))
    decay_t = jnp.exp(jnp.where(ci >= ri, row - col, -jnp.inf))
    kb = k * beta
    kbf = k.astype(BF16)
    kk = _bmm_nt(kbf, kbf)
    a = jnp.where(ri > ci, kk * decay, 0.0) * beta
    at = jnp.where(ci > ri, kk * decay_t, 0.0) * beta_row
    inv = _unit_lower_inverse(a, at, ri, ci)
    eg = jnp.exp(col)
    rhs = jnp.concatenate([kb * eg, v * beta], axis=-1).astype(BF16)
    sol = _bmm(inv.astype(BF16), rhs)
    w, u = sol[:, :, :GDN_DK].astype(BF16), sol[:, :, GDN_DK:]
    attn = (_bmm_nt(q.astype(BF16), kbf) * decay).astype(BF16)
    g_last = col[:, CHUNK - 1:CHUNK, :]
    q_dec = (q * eg).astype(BF16)
    k_dec = (k * jnp.exp(g_last - col)).astype(BF16)
    e_last = jnp.exp(g_last)
    st = state_ref[...]
    for c in range(nchunk):
        sl = slice(c * GDN_HEADS, (c + 1) * GDN_HEADS)
        stb = st.astype(BF16)
        v_new = u[sl] - _bmm(w[sl], stb)
        vnb = v_new.astype(BF16)
        o = _bmm(q_dec[sl], stb) + _bmm(attn[sl], vnb)
        st = st * e_last[sl] + jnp.einsum('hik,hiv->hkv', k_dec[sl], vnb, preferred_element_type=F32)
        o = o * lax.rsqrt(_lane_sum(o * o) * (1.0 / GDN_DV) + NORM_EPS)
        for hd in range(GDN_HEADS):
            gate = z_ref[c * CHUNK:(c + 1) * CHUNK, hd * GDN_DV:(hd + 1) * GDN_DV].astype(F32)
            o_ref[c * CHUNK:(c + 1) * CHUNK, hd * GDN_DV:(hd + 1) * GDN_DV] = (
                o[hd] * nw_ref[...] * gate).astype(BF16)
    state_ref[...] = st


def _pad_lanes(v, start):
    return jnp.zeros((SMALL_COLS,), F32).at[start:start + v.shape[0]].set(v)


def _chunk_tri(ts):
    return jnp.kron(jnp.eye(ts // CHUNK, dtype=F32), jnp.tril(jnp.ones((CHUNK, CHUNK), F32))).astype(BF16)


def _gdn_mixer(ts, a_log, dt_bias, norm_w):
    prm = jnp.stack([_pad_lanes(a_log, SMALL_A), _pad_lanes(dt_bias, SMALL_A)])
    consts = [prm, norm_w.reshape(1, -1), _chunk_tri(ts)]
    cols = [(COL_QKV, 3 * GDN_QK), (COL_ZGDN, GDN_WIDTH)]
    return dict(body=_gdn_kernel, cols=cols, small=True, consts=consts, width=GDN_WIDTH,
                scratch=pltpu.VMEM((GDN_HEADS, GDN_DK, GDN_DV), F32))


def _ssd_kernel(x_ref, z_ref, bc_ref, small_ref, prm_ref, hv_ref, tri_ref, expand_ref, smask_ref, o_ref,
                state_ref):
    x = x_ref[...].astype(F32)
    ts = x.shape[0]
    bmb, cmb = bc_ref[:, :SSD_BC], bc_ref[:, SSD_BC:]
    cm = cmb.astype(F32)
    sm = small_ref[...]
    dt_all = _softplus(sm + prm_ref[1:2, :])
    la_all = dt_all * (-jnp.exp(prm_ref[0:1, :]))
    cs = _select_rows(tri_ref[...], la_all)
    cst = cs.T
    expand = expand_ref[...]
    dtx = _select_cols(dt_all, expand)
    csx = _select_cols(cs, expand)
    xdt = x * dtx
    xdtb = xdt.astype(BF16)
    ri = lax.broadcasted_iota(jnp.int32, (CHUNK, CHUNK), 0)
    ci = lax.broadcasted_iota(jnp.int32, (CHUNK, CHUNK), 1)
    lane = lax.broadcasted_iota(jnp.int32, (CHUNK, SSD_BC), 1)
    within, updates, lasts = [], [], []
    for c in range(ts // CHUNK):
        rs = slice(c * CHUNK, (c + 1) * CHUNK)
        cb = []
        for g in range(SSD_GROUPS):
            in_g = (lane >= g * SSD_STATE) & (lane < (g + 1) * SSD_STATE)
            cb.append(_dot_nt(jnp.where(in_g, cm[rs, :], 0.0).astype(BF16), bmb[rs, :]))
        halves = []
        for pair in range(SSD_HEADS // 2):
            rhs = xdtb[rs, pair * LANES:(pair + 1) * LANES]
            ys = []
            for hd in (2 * pair, 2 * pair + 1):
                col = cs[rs, SMALL_DT + hd:SMALL_DT + hd + 1]
                row = cst[SMALL_DT + hd:SMALL_DT + hd + 1, rs]
                decay = jnp.exp(jnp.where(ri >= ci, col - row, -jnp.inf))
                ys.append(_dot((cb[hd // SSD_HPG] * decay).astype(BF16), rhs))
            halves.append(jnp.where(lane < SSD_HEAD_DIM, ys[0], ys[1]))
        within.append(jnp.concatenate(halves, axis=-1))
        cs_last_x = csx[(c + 1) * CHUNK - 1:(c + 1) * CHUNK, :]
        lasts.append(cs_last_x)
        to_end = jnp.exp(cs_last_x - csx[rs, :])
        updates.append(_dot_tn(bmb[rs, :], (xdt[rs, :] * to_end).astype(BF16)) * smask_ref[...])
    st = state_ref[...]
    ecs = jnp.exp(csx)
    parts = []
    for c in range(ts // CHUNK):
        rs = slice(c * CHUNK, (c + 1) * CHUNK)
        parts.append(within[c] + _dot(cmb[rs, :], st.astype(BF16)) * ecs[rs, :])
        st = st * jnp.exp(lasts[c]) + updates[c]
    state_ref[...] = st
    y = jnp.concatenate(parts, axis=0) + hv_ref[0:1, :] * x
    y = y * z_ref[...].astype(F32)
    y = y * lax.rsqrt(jnp.mean(y * y, axis=-1, keepdims=True) + NORM_EPS)
    o_ref[...] = (y * hv_ref[1:2, :]).astype(BF16)


def _ssd_mixer(ts, a_log, dt_bias, d, norm_w):
    prm = jnp.stack([_pad_lanes(a_log, SMALL_DT), _pad_lanes(dt_bias, SMALL_DT)])
    hv = jnp.stack([jnp.repeat(d, SSD_HEAD_DIM), norm_w])
    lane_head = jnp.arange(SSD_WIDTH) // SSD_HEAD_DIM
    expand = (jnp.arange(SMALL_COLS)[:, None] == SMALL_DT + lane_head[None, :]).astype(BF16)
    row_group = jnp.arange(SSD_BC) // SSD_STATE
    smask = (row_group[:, None] == (lane_head // SSD_HPG)[None, :]).astype(F32)
    consts = [prm, hv, _chunk_tri(ts), expand, smask]
    cols = [(COL_XSSD, SSD_WIDTH), (COL_ZSSD, SSD_WIDTH), (COL_BC, 2 * SSD_BC)]
    return dict(body=_ssd_kernel, cols=cols, small=True, consts=consts, width=SSD_WIDTH,
                scratch=pltpu.VMEM((SSD_BC, SSD_WIDTH), F32))


def _lru_kernel(x_ref, g_ref, wr_ref, wi_ref, vec_ref, o_ref, h_ref):
    step = pl.program_id(1)

    xb = x_ref[...]
    x = xb.astype(F32)
    ts = x.shape[0]
    r = _sigmoid(_dot(xb, wr_ref[...]) + vec_ref[0:1, :])
    gate_i = _sigmoid(_dot(xb, wi_ref[...]) + vec_ref[1:2, :])
    log_a = -LRU_C * r * _softplus(-vec_ref[2:3, :])
    a = jnp.exp(log_a)
    one_minus_a2 = -jnp.tanh(log_a) * (a * a + 1.0)
    mult = one_minus_a2 * lax.rsqrt(jnp.maximum(one_minus_a2, 1e-30))
    row = lax.broadcasted_iota(jnp.int32, (ts, LRU_WIDTH), 0)
    mult = jnp.where(row + step * ts == 0, 1.0, mult)
    b = mult * gate_i * x

    row_in = row % SUBLANES
    shift = 1
    while shift < SUBLANES:
        keep = row_in >= shift
        a_prev = jnp.where(keep, pltpu.roll(a, shift, 0), 1.0)
        b_prev = jnp.where(keep, pltpu.roll(b, shift, 0), 0.0)
        b = a * b_prev + b
        a = a * a_prev
        shift *= 2
    carry = h_ref[...]
    blocks = []
    for blk in range(ts // SUBLANES):
        rows = slice(blk * SUBLANES, (blk + 1) * SUBLANES)
        blocks.append(b[rows, :] + a[rows, :] * carry)
        carry = blocks[-1][SUBLANES - 1:SUBLANES, :]
    h_ref[...] = carry
    o_ref[...] = (jnp.concatenate(blocks, axis=0) * g_ref[...].astype(F32)).astype(BF16)


def _lru_mixer(lam, wr, br, wi, bi):
    eye = jnp.eye(LRU_BLOCKS, dtype=F32)

    def blockdiag(w):
        return jnp.einsum('nde,nm->ndme', w, eye).reshape(LRU_WIDTH, LRU_WIDTH).astype(BF16)

    consts = [blockdiag(wr), blockdiag(wi), jnp.stack([br, bi, lam])]
    cols = [(COL_XLRU, LRU_WIDTH), (COL_GLRU, LRU_WIDTH)]
    return dict(body=_lru_kernel, cols=cols, small=False, consts=consts, width=LRU_WIDTH,
                scratch=pltpu.VMEM((1, LRU_WIDTH), F32))


def _mixers_kernel(*refs, layout):
    n_in = sum(n for _, n in layout)
    nmix = len(layout)
    gm_ref, x_ref, mod_ref = refs[n_in:n_in + 3]
    w_refs = refs[n_in + 3:n_in + 3 + nmix]
    wo_ref, o_ref = refs[n_in + 3 + nmix:n_in + 5 + nmix]
    states = refs[n_in + 5 + nmix:n_in + 5 + 2 * nmix]
    ybufs = refs[n_in + 5 + 2 * nmix:]

    @pl.when(pl.program_id(1) == 0)
    def _():
        for st in states:
            st[...] = jnp.zeros_like(st)

    pos = 0
    merged = None
    for i, ((body, n), ybuf, st, w_ref) in enumerate(zip(layout, ybufs, states, w_refs)):
        body(*refs[pos:pos + n], ybuf, st)
        pos += n
        term = gm_ref[:, i * D_MODEL:(i + 1) * D_MODEL].astype(F32) * _dot(ybuf[...], w_ref[...])
        merged = term if merged is None else merged + term
    out = _dot(merged.astype(BF16), wo_ref[...])
    o_ref[...] = x_ref[...] + mod_ref[:, 2 * D_MODEL:3 * D_MODEL] * out


def _mixers_merge(p_main, small, x2, mod3, w_branch, w_out, bsz, seq, parts):
    ts = min(MIX_TILE, seq)
    nsb = seq // ts

    def rows(b, s):
        return b * nsb + s

    operands, in_specs, layout = [], [], []
    for part in parts:
        for off, width in part["cols"]:
            operands.append(p_main)
            in_specs.append(pl.BlockSpec((ts, width), functools.partial(
                lambda b, s, col: (rows(b, s), col), col=off // width)))
        if part["small"]:
            operands.append(small)
            in_specs.append(pl.BlockSpec((ts, SMALL_COLS), lambda b, s: (rows(b, s), 0)))
        operands += part["consts"]
        in_specs += [_full(a.shape) for a in part["consts"]]
        layout.append((part["body"], len(part["cols"]) + int(part["small"]) + len(part["consts"])))
    gw = N_BRANCH * D_MODEL
    operands += [p_main, x2, mod3]
    in_specs += [
        pl.BlockSpec((ts, gw), lambda b, s: (rows(b, s), COL_GMERGE // gw)),
        pl.BlockSpec((ts, D_MODEL), lambda b, s: (rows(b, s), 0)),
        pl.BlockSpec((None, 1, 6 * D_MODEL), lambda b, s: (b, 0, 0)),
    ]
    acc = 0
    for part in parts:
        operands.append(w_branch[acc:acc + part["width"]].astype(BF16))
        in_specs.append(_full((part["width"], D_MODEL)))
        acc += part["width"]
    operands.append(w_out.astype(BF16))
    in_specs.append(_full((D_MODEL, D_MODEL)))
    return pl.pallas_call(
        functools.partial(_mixers_kernel, layout=tuple(layout)),
        grid=(bsz, nsb),
        in_specs=in_specs,
        out_specs=pl.BlockSpec((ts, D_MODEL), lambda b, s: (rows(b, s), 0)),
        out_shape=jax.ShapeDtypeStruct((bsz * seq, D_MODEL), F32),
        scratch_shapes=[part["scratch"] for part in parts]
        + [pltpu.VMEM((ts, part["width"]), BF16) for part in parts],
        compiler_params=_params("parallel", "arbitrary"),
        name="mixers_merge",
    )(*operands)


FFN_SLICES = tuple((s, min(512, FFN_HIDDEN - s)) for s in range(0, FFN_HIDDEN, 512))


def _ffn_kernel(x_ref, mod_ref, g_ref, w13_ref, w2_ref, gf_ref, o_ref, *, final):
    x = x_ref[...]
    h = _modulated_norm(x, g_ref[...], mod_ref[:, 3 * D_MODEL:4 * D_MODEL], mod_ref[:, 4 * D_MODEL:5 * D_MODEL])
    hb = h.astype(BF16)
    acc = None
    for start, size in FFN_SLICES:
        a = _dot(hb, w13_ref[:, start:start + size])
        b = _dot(hb, w13_ref[:, FFN_HIDDEN + start:FFN_HIDDEN + start + size])
        part = _dot((_silu(a) * b).astype(BF16), w2_ref[start:start + size, :])
        acc = part if acc is None else acc + part
    x = x + mod_ref[:, 5 * D_MODEL:6 * D_MODEL] * acc
    if final:
        x = x * lax.rsqrt(jnp.mean(x * x, axis=-1, keepdims=True) + NORM_EPS) * gf_ref[...]
    o_ref[...] = x


def _ffn(x2, mod3, ln_g, w13, w2, ln_final, seq, final):
    t = x2.shape[0]
    tm = min(512, seq)
    per_b = seq // tm
    return pl.pallas_call(
        functools.partial(_ffn_kernel, final=final),
        grid=(t // tm,),
        in_specs=[
            pl.BlockSpec((tm, D_MODEL), lambda i: (i, 0)),
            pl.BlockSpec((None, 1, 6 * D_MODEL), lambda i: (i // per_b, 0, 0)),
            _full((1, D_MODEL)),
            pl.BlockSpec((D_MODEL, 2 * FFN_HIDDEN), lambda i: (0, 0), pipeline_mode=pl.Buffered(1)),
            pl.BlockSpec((FFN_HIDDEN, D_MODEL), lambda i: (0, 0), pipeline_mode=pl.Buffered(1)),
            _full((1, D_MODEL)),
        ],
        out_specs=pl.BlockSpec((tm, D_MODEL), lambda i: (i, 0)),
        out_shape=jax.ShapeDtypeStruct((t, D_MODEL), F32),
        compiler_params=_params("parallel"),
        name="ffn_final" if final else "ffn",
    )(x2, mod3, ln_g, w13, w2, ln_final)


def _split_in_weights(w_in, conv_w, conv_b):
    o = 0
    seg = {}
    for name, width in (("q", 512), ("k", 512), ("v", 512), ("xssd", 512), ("bssd", 128), ("cssd", 128),
                        ("xlru", 512), ("us5", 384), ("bgdn", 4), ("agdn", 4), ("zgdn", 512), ("zssd", 512),
                        ("dtssd", 8), ("glru", 512), ("gmerge", 4096)):
        seg[name] = (o, o + width)
        o += width

    def cols(arr, names):
        return jnp.concatenate([arr[:, seg[n][0]:seg[n][1]] for n in names], axis=1)

    conv_names = ("q", "k", "v", "xssd", "xlru", "bssd", "cssd")
    us5 = w_in[:, seg["us5"][0]:seg["us5"][1]]
    main = jnp.concatenate([cols(w_in, conv_names), us5[:, :2 * LANES], cols(w_in, ("zgdn", "zssd", "gmerge", "glru")),
                            us5[:, 2 * LANES:]], axis=1).astype(BF16)
    pad = jnp.zeros((w_in.shape[0], SMALL_COLS - 16), w_in.dtype)
    small = jnp.concatenate([cols(w_in, ("bgdn", "agdn", "dtssd")), pad], axis=1).astype(BF16)
    return main, small, cols(conv_w, conv_names), cols(conv_b.reshape(1, -1), conv_names)


def kernel(x, c, ln_mix_g, ln_ffn_g, ln_final_g, ada_w, ada_b, w_in, conv_w, conv_b, s5_lambda_re, s5_lambda_im, s5_log_dt, s5_b_re, s5_b_im, s5_c_re, s5_c_im, s5_d, s5_glu_w, s5_glu_b, gdn_a_log, gdn_dt_bias, gdn_norm_w, ssd_a_log, ssd_dt_bias, ssd_d, ssd_norm_w, lru_lambda, lru_wr, lru_br, lru_wi, lru_bi, w_branch, w_out, ffn_w13, ffn_w2):
    bsz, seq, d = x.shape
    t = bsz * seq
    x2 = x.reshape(t, d)
    mod = _ada_mod(c, ada_w, ada_b)
    ln_final = ln_final_g.reshape(1, d)
    for l in range(DEPTH):
        mod3 = mod[l].reshape(bsz, 1, 6 * d)
        w_main, w_small, cw, cb = _split_in_weights(w_in[l], conv_w[l], conv_b[l])
        p_main, small = _inproj(x2, mod3, ln_mix_g[l].reshape(1, d), w_main, w_small, cw, cb, seq)

        tables = _s5_tables(s5_lambda_re[l], s5_lambda_im[l], s5_log_dt[l], s5_b_re[l], s5_b_im[l],
                            s5_c_re[l], s5_c_im[l], min(CHUNK, seq))
        mix_ts = min(MIX_TILE, seq)
        x2 = _mixers_merge(p_main, small, x2, mod3, w_branch[l], w_out[l], bsz, seq, (
            _s5_mixer(tables, s5_d[l], s5_glu_w[l], s5_glu_b[l]),
            _gdn_mixer(mix_ts, gdn_a_log[l], gdn_dt_bias[l], gdn_norm_w[l]),
            _ssd_mixer(mix_ts, ssd_a_log[l], ssd_dt_bias[l], ssd_d[l], ssd_norm_w[l]),
            _lru_mixer(lru_lambda[l], lru_wr[l], lru_br[l], lru_wi[l], lru_bi[l])))
        x2 = _ffn(x2, mod3, ln_ffn_g[l].reshape(1, d), ffn_w13[l].astype(BF16), ffn_w2[l].astype(BF16),
                  ln_final, seq, final=(l == DEPTH - 1))
    return x2.reshape(bsz, seq, d)
```

```python
import functools

import jax
import jax.numpy as jnp
from jax import lax
from jax.experimental import pallas as pl
from jax.experimental.pallas import tpu as pltpu

F32 = jnp.float32
BF16 = jnp.bfloat16

D_MODEL = 1024
DEPTH = 2
CONV_K = 4
NORM_EPS = 1e-6
S5_GROUP, S5_STATE, S5_WIDTH, S5_GROUPS = 16, 64, 384, 24
GDN_HEADS, GDN_DK, GDN_DV = 4, 128, 128
GDN_QK = GDN_HEADS * GDN_DK
GDN_WIDTH = GDN_HEADS * GDN_DV
SSD_HEAD_DIM, SSD_WIDTH, SSD_HEADS, SSD_GROUPS, SSD_HPG, SSD_STATE = 64, 512, 8, 2, 4, 64
SSD_BC = SSD_GROUPS * SSD_STATE
LRU_WIDTH, LRU_BLOCK, LRU_BLOCKS, LRU_C = 512, 64, 8, 8.0
N_BRANCH = 4
FFN_HIDDEN = 2816

LANES = 128
SUBLANES = 8

COL_QKV, COL_XSSD, COL_XLRU, COL_BC = 0, 1536, 2048, 2560
CONV_COLS = 2816
SILU_CONV_END = 2048
COL_US5A, COL_ZGDN, COL_ZSSD, COL_GMERGE, COL_GLRU, COL_US5B = 2816, 3072, 3584, 4096, 8192, 8704
MAIN_COLS = 8832
PROJ_TILES = 3
PROJ_TN = MAIN_COLS // PROJ_TILES
PROJ_ROW_CHUNKS = 4
SMALL_COLS = 128
SMALL_B, SMALL_A, SMALL_DT = 0, 4, 8

CHUNK = 128
MIX_TILE = 256
S5_PARTS = S5_WIDTH // LANES
S5_PART_STATE = (S5_GROUPS // S5_PARTS) * S5_STATE
VMEM_LIMIT = 56 * 1024 * 1024


def _sigmoid(x):
    return 0.5 * jnp.tanh(0.5 * x) + 0.5


def _silu(x):
    return x * _sigmoid(x)


def _softplus(x):
    return jnp.maximum(x, 0.0) + jnp.log1p(jnp.exp(-jnp.abs(x)))


def _gelu(x):
    return jax.nn.gelu(x, approximate=True)


def _dot(a, b):
    return jnp.dot(a, b, preferred_element_type=F32)


def _dot_nt(a, b):
    return lax.dot_general(a, b, (((1,), (1,)), ((), ())), preferred_element_type=F32)


def _dot_tn(a, b):
    return lax.dot_general(a, b, (((0,), (0,)), ((), ())), preferred_element_type=F32)


def _bmm(a, b):
    return jnp.einsum('gij,gjk->gik', a, b, preferred_element_type=F32)


def _bmm_nt(a, b):
    return jnp.einsum('gik,gjk->gij', a, b, preferred_element_type=F32)


def _split3(x):
    x1 = x.astype(BF16)
    r1 = x - x1.astype(F32)
    x2 = r1.astype(BF16)
    x3 = (r1 - x2.astype(F32)).astype(BF16)
    return x1, x2, x3


def _select_rows(sel_bf16, x):
    x1, x2, x3 = _split3(x)
    return _dot(sel_bf16, x1) + (_dot(sel_bf16, x2) + _dot(sel_bf16, x3))


def _select_cols(x, sel_bf16):
    x1, x2, x3 = _split3(x)
    return _dot(x1, sel_bf16) + (_dot(x2, sel_bf16) + _dot(x3, sel_bf16))


def _lane_sum(x):
    shape = x.shape
    ones = jnp.ones((shape[-1], shape[-1]), BF16)
    return _dot(x.reshape(-1, shape[-1]).astype(BF16), ones).reshape(shape)


def _params(*sem):
    return pltpu.CompilerParams(dimension_semantics=sem, vmem_limit_bytes=VMEM_LIMIT)


def _full(shape):
    n = len(shape)
    return pl.BlockSpec(shape, lambda *_: (0,) * n)


def _ada_kernel(c_ref, w_ref, b_ref, o_ref):
    cond = _silu(c_ref[...]).astype(BF16)
    o_ref[...] = _dot(cond, w_ref[...].astype(BF16)) + b_ref[...]


def _ada_mod(c, ada_w, ada_b):
    depth, d, n = ada_w.shape
    bsz = c.shape[0]
    tn = 1536
    return pl.pallas_call(
        _ada_kernel,
        grid=(depth, n // tn),
        in_specs=[
            pl.BlockSpec((bsz, d), lambda l, j: (0, 0)),
            pl.BlockSpec((None, d, tn), lambda l, j: (l, 0, j)),
            pl.BlockSpec((None, 1, tn), lambda l, j: (l, 0, j)),
        ],
        out_specs=pl.BlockSpec((None, bsz, tn), lambda l, j: (l, 0, j)),
        out_shape=jax.ShapeDtypeStruct((depth, bsz, n), F32),
        compiler_params=_params("parallel", "parallel"),
        name="ada_mod",
    )(c, ada_w, ada_b.reshape(depth, 1, n))


def _modulated_norm(x, g, shift, scale):
    ms = jnp.mean(x * x, axis=-1, keepdims=True)
    return x * lax.rsqrt(ms + NORM_EPS) * g * (1.0 + scale) + shift


CONV_STRIDE = 4


def _causal_conv(x, xpad_ref, stage_ref, w_ref, b_ref, emit):
    ts, c = x.shape
    span = CONV_STRIDE * SUBLANES

    for tile in range(c // LANES):
        lanes = slice(tile * LANES, (tile + 1) * LANES)
        xt = x[:, lanes]
        xpad_ref[tile, SUBLANES:SUBLANES + ts, :] = xt
        taps = [jnp.broadcast_to(w_ref[k:k + 1, lanes], (SUBLANES, LANES)) for k in range(CONV_K)]
        bias = jnp.broadcast_to(b_ref[:, lanes], (SUBLANES, LANES))
        for group in range(ts // span):
            for res in range(CONV_STRIDE):
                row0 = group * span + res
                acc = bias
                for k in range(CONV_K):
                    start = SUBLANES + row0 - (CONV_K - 1) + k
                    acc = acc + taps[k] * xpad_ref[tile, pl.ds(start, SUBLANES, stride=CONV_STRIDE), :]
                stage_ref[tile, pl.ds(row0, SUBLANES, stride=CONV_STRIDE), :] = acc
        xpad_ref[tile, 0:SUBLANES, :] = xt[ts - SUBLANES:, :]
        emit(tile, stage_ref[tile])


def _inproj_kernel(x_ref, mod_ref, g_ref, w_ref, ws_ref, cw_ref, cb_ref, p_ref, small_ref, xpad_ref, stage_ref, *,
                   per_b):
    rows = x_ref.shape[0] // PROJ_ROW_CHUNKS

    @pl.when(pl.program_id(0) % per_b == 0)
    def _():
        xpad_ref[:, 0:SUBLANES, :] = jnp.zeros((CONV_COLS // LANES, SUBLANES, LANES), F32)

    def column_tile_0(r, put):
        def emit(tile, y):
            lo = tile * LANES
            plain = SILU_CONV_END <= lo < COL_BC
            put(lo, lo + LANES, y if plain else _silu(y))

        _causal_conv(r[:, :CONV_COLS], xpad_ref, stage_ref, cw_ref, cb_ref, emit)
        put(CONV_COLS, PROJ_TN, r[:, CONV_COLS:])

    def column_tile_1(r, put):
        z0, g0 = COL_ZGDN - PROJ_TN, COL_GMERGE - PROJ_TN
        put(PROJ_TN, PROJ_TN + z0, r[:, :z0])
        put(PROJ_TN + z0, PROJ_TN + g0, _silu(r[:, z0:g0]))
        put(PROJ_TN + g0, 2 * PROJ_TN, _sigmoid(r[:, g0:]))

    def column_tile_2(r, put):
        l0, u0 = COL_GLRU - 2 * PROJ_TN, COL_US5B - 2 * PROJ_TN
        put(2 * PROJ_TN, 2 * PROJ_TN + l0, _sigmoid(r[:, :l0]))
        put(2 * PROJ_TN + l0, 2 * PROJ_TN + u0, _gelu(r[:, l0:u0]))
        put(2 * PROJ_TN + u0, 3 * PROJ_TN, r[:, u0:])

    for rc in range(PROJ_ROW_CHUNKS):
        r0 = rc * rows
        h = _modulated_norm(x_ref[r0:r0 + rows, :], g_ref[...], mod_ref[:, 0:D_MODEL],
                            mod_ref[:, D_MODEL:2 * D_MODEL])
        hb = h.astype(BF16)
        small_ref[r0:r0 + rows, :] = _dot(hb, ws_ref[...])

        def put(lo, hi, val, r0=r0):
            p_ref[r0:r0 + rows, lo:hi] = val.astype(BF16)

        for jt, epilogue in enumerate((column_tile_0, column_tile_1, column_tile_2)):
            epilogue(_dot(hb, w_ref[:, jt * PROJ_TN:(jt + 1) * PROJ_TN]), put)


def _inproj(x2, mod3, ln_g, w_main, w_small, conv_w, conv_b, seq):
    t = x2.shape[0]
    tm = min(512, seq)
    per_b = seq // tm
    return pl.pallas_call(
        functools.partial(_inproj_kernel, per_b=per_b),
        grid=(t // tm,),
        in_specs=[
            pl.BlockSpec((tm, D_MODEL), lambda i: (i, 0)),
            pl.BlockSpec((None, 1, 6 * D_MODEL), lambda i: (i // per_b, 0, 0)),
            pl.BlockSpec((1, D_MODEL), lambda i: (0, 0)),
            pl.BlockSpec((D_MODEL, MAIN_COLS), lambda i: (0, 0), pipeline_mode=pl.Buffered(1)),
            pl.BlockSpec((D_MODEL, SMALL_COLS), lambda i: (0, 0)),
            pl.BlockSpec((CONV_K, CONV_COLS), lambda i: (0, 0)),
            pl.BlockSpec((1, CONV_COLS), lambda i: (0, 0)),
        ],
        out_specs=[
            pl.BlockSpec((tm, MAIN_COLS), lambda i: (i, 0)),
            pl.BlockSpec((tm, SMALL_COLS), lambda i: (i, 0)),
        ],
        out_shape=[
            jax.ShapeDtypeStruct((t, MAIN_COLS), BF16),
            jax.ShapeDtypeStruct((t, SMALL_COLS), F32),
        ],
        scratch_shapes=[pltpu.VMEM((CONV_COLS // LANES, tm // PROJ_ROW_CHUNKS + SUBLANES, LANES), F32),
                        pltpu.VMEM((CONV_COLS // LANES, tm // PROJ_ROW_CHUNKS, LANES), F32)],
        compiler_params=_params("arbitrary"),
        name="inproj",
    )(x2, mod3, ln_g, w_main, w_small, conv_w, conv_b)


def _s5_kernel(u0_ref, u1_ref, u2_ref, wb_ref, pneg_ref, ppos_ref, lam1_ref, tri_ref, wc_ref, d_ref, gw_ref,
               gb_ref, o_ref, carry_ref):
    h = S5_PART_STATE
    nchunk = u0_ref.shape[0] // CHUNK
    tri = tri_ref[...]
    ys = [[None] * S5_PARTS for _ in range(nchunk)]
    for part, u_ref in enumerate((u0_ref, u1_ref, u2_ref)):
        lo = part * 2 * h
        ub = u_ref[...]
        bu = _dot(ub, wb_ref[part])
        nr, ni = pneg_ref[:, lo:lo + h], pneg_ref[:, lo + h:lo + 2 * h]
        sums = []
        for c in range(nchunk):
            br, bi = bu[c * CHUNK:(c + 1) * CHUNK, :h], bu[c * CHUNK:(c + 1) * CHUNK, h:]
            z = jnp.concatenate([nr * br - ni * bi, nr * bi + ni * br], axis=-1).astype(BF16)
            sums.append(_dot(tri, z))
        cr, ci = carry_ref[:, lo:lo + h], carry_ref[:, lo + h:lo + 2 * h]
        l1r, l1i = lam1_ref[:, lo:lo + h], lam1_ref[:, lo + h:lo + 2 * h]
        pr, pi = ppos_ref[:, lo:lo + h], ppos_ref[:, lo + h:lo + 2 * h]
        for c in range(nchunk):
            s = sums[c]
            sr = s[:, :h] + (l1r * cr - l1i * ci)
            si = s[:, h:] + (l1r * ci + l1i * cr)
            xr = pr * sr - pi * si
            xi = pr * si + pi * sr
            cr, ci = xr[CHUNK - 1:CHUNK, :], xi[CHUNK - 1:CHUNK, :]
            xcat = jnp.concatenate([xr, xi], axis=-1).astype(BF16)
            skip = d_ref[:, part * LANES:(part + 1) * LANES] * ub[c * CHUNK:(c + 1) * CHUNK, :].astype(F32)
            ys[c][part] = _dot(xcat, wc_ref[part]) + skip
        carry_ref[:, lo:lo + h] = cr
        carry_ref[:, lo + h:lo + 2 * h] = ci
    y = _gelu(jnp.concatenate([jnp.concatenate(row, axis=-1) for row in ys], axis=0))
    o_ref[...] = (y * _sigmoid(_dot(y.astype(BF16), gw_ref[...]) + gb_ref[...])).astype(BF16)


def _s5_tables(lam_re, lam_im, log_dt, b_re, b_im, c_re, c_im, ts):
    gp = S5_GROUPS // S5_PARTS
    dt = jnp.exp(log_dt)[:, None]
    ar, ai = lam_re * dt, lam_im * dt
    er = jnp.exp(ar)
    lbr, lbi = er * jnp.cos(ai), er * jnp.sin(ai)
    den = lam_re * lam_re + lam_im * lam_im
    qr = ((lbr - 1.0) * lam_re + lbi * lam_im) / den
    qi = (lbi * lam_re - (lbr - 1.0) * lam_im) / den
    bbr = qr[..., None] * b_re - qi[..., None] * b_im
    bbi = qr[..., None] * b_im + qi[..., None] * b_re
    eye = jnp.eye(gp, dtype=F32)

    def blockdiag_in(m):
        m = m.reshape(S5_PARTS, gp, S5_STATE, S5_GROUP)
        return jnp.einsum('qgph,gk->qghkp', m, eye).reshape(S5_PARTS, LANES, S5_PART_STATE)

    def blockdiag_out(m):
        m = m.reshape(S5_PARTS, gp, S5_GROUP, S5_STATE)
        return jnp.einsum('qghp,gk->qgpkh', m, eye).reshape(S5_PARTS, S5_PART_STATE, LANES)

    wb = jnp.concatenate([blockdiag_in(bbr), blockdiag_in(bbi)], axis=2).astype(BF16)
    wc = jnp.concatenate([blockdiag_out(c_re), -blockdiag_out(c_im)], axis=1).astype(BF16)

    def lanes(re, im):
        lead = re.shape[:-2]
        re = re.reshape(lead + (S5_PARTS, 1, S5_PART_STATE))
        im = im.reshape(lead + (S5_PARTS, 1, S5_PART_STATE))
        return jnp.concatenate([re, im], axis=-2).reshape(lead + (S5_PARTS * 2 * S5_PART_STATE,))

    j = jnp.arange(ts, dtype=F32)[:, None, None]

    def power(sign):
        mag = jnp.exp(sign * j * ar[None])
        ang = sign * j * ai[None]
        return lanes(mag * jnp.cos(ang), mag * jnp.sin(ang))

    return wb, power(-1.0), power(1.0), lanes(lbr, lbi)[None, :], wc


def _s5_mixer(tables, d, glu_w, glu_b):
    wb, pneg, ppos, lam1, wc = tables
    tri = jnp.tril(jnp.ones((CHUNK, CHUNK), F32)).astype(BF16)
    consts = [wb, pneg, ppos, lam1, tri, wc, d.reshape(1, -1), glu_w.astype(BF16), glu_b.reshape(1, -1)]
    cols = [(COL_US5A, LANES), (COL_US5A + LANES, LANES), (COL_US5B, LANES)]
    return dict(body=_s5_kernel, cols=cols, small=False, consts=consts, width=S5_WIDTH,
                scratch=pltpu.VMEM((1, S5_PARTS * 2 * S5_PART_STATE), F32))


INV_BASE = 16


def _block_inverse_rows(mt):
    groups, n, w = mt.shape
    mt2 = mt.reshape(groups * n, w)
    sub = lax.broadcasted_iota(jnp.int32, (groups * n, w), 0) % n
    lane = lax.broadcasted_iota(jnp.int32, (groups * n, w), 1)
    lane_in = lane % n
    lane_base = lane - lane_in
    x = (sub == lane_in).astype(F32)
    for i in range(1, n):
        coef = jnp.take_along_axis(mt2, lane_base + i, axis=1)
        acc = jnp.sum((coef * x).reshape(groups, n, w), axis=1, keepdims=True)
        new_row = jnp.where(lane_in[0:1, :] == i, 1.0, 0.0) - acc
        new_rows = jnp.broadcast_to(new_row, (groups, n, w)).reshape(groups * n, w)
        x = jnp.where(sub == i, new_rows, x)
    return x.reshape(groups, n, w)


def _unit_lower_inverse(a, at, ri, ci):
    n = a.shape[1]
    assert n == LANES
    diag_blk = ri // INV_BASE == ci // INV_BASE
    at_blk = jnp.where(diag_blk, at, 0.0)
    mt = at_blk[:, 0:INV_BASE, :]
    for blk in range(1, n // INV_BASE):
        mt = mt + at_blk[:, blk * INV_BASE:(blk + 1) * INV_BASE, :]
    x = _block_inverse_rows(mt)
    inv = jnp.where(diag_blk, jnp.concatenate([x] * (n // INV_BASE), axis=1), 0.0)
    size = INV_BASE
    while size < n:
        lower_left = (ri // (2 * size) == ci // (2 * size)) & ((ri // size) % 2 == 1) & ((ci // size) % 2 == 0)
        invb = inv.astype(BF16)
        inv = inv - _bmm(_bmm(invb, jnp.where(lower_left, a, 0.0).astype(BF16)).astype(BF16), invb)
        size *= 2
    return inv


def _gdn_kernel(qkv_ref, z_ref, small_ref, prm_ref, nw_ref, tri_ref, o_ref, state_ref):
    ts = qkv_ref.shape[0]
    sm = small_ref[...]
    beta_all = _sigmoid(sm)
    g_all = -jnp.exp(prm_ref[0:1, :]) * _softplus(sm + prm_ref[1:2, :])
    gc = _select_rows(tri_ref[...], g_all)
    small_lane = lax.broadcasted_iota(jnp.int32, sm.shape, 1)
    rows_t = jnp.where(small_lane < SMALL_A, beta_all, gc).T
    nchunk = ts // CHUNK
    ri = lax.broadcasted_iota(jnp.int32, (CHUNK, CHUNK), 0)
    ci = lax.broadcasted_iota(jnp.int32, (CHUNK, CHUNK), 1)
    groups = [(c, hd) for c in range(nchunk) for hd in range(GDN_HEADS)]

    def head_tiles(off):
        return jnp.stack([qkv_ref[c * CHUNK:(c + 1) * CHUNK, off + hd * GDN_DK:off + (hd + 1) * GDN_DK]
                          for c, hd in groups]).astype(F32)

    def head_cols(arr, lane0):
        return jnp.stack([arr[c * CHUNK:(c + 1) * CHUNK, lane0 + hd:lane0 + hd + 1] for c, hd in groups])

    def head_rows(lane0):
        return jnp.stack([rows_t[lane0 + hd:lane0 + hd + 1, c * CHUNK:(c + 1) * CHUNK] for c, hd in groups])

    q = head_tiles(0)
    k = head_tiles(GDN_QK)
    v = head_tiles(2 * GDN_QK)
    q = q * lax.rsqrt(_lane_sum(q * q) + NORM_EPS) * (GDN_DK ** -0.5)
    k = k * lax.rsqrt(_lane_sum(k * k) + NORM_EPS)
    wide = (len(groups), CHUNK, LANES)
    col, row = jnp.broadcast_to(head_cols(gc, SMALL_A), wide), head_rows(SMALL_A)
    beta, beta_row = jnp.broadcast_to(head_cols(beta_all, SMALL_B), wide), head_rows(SMALL_B)
    decay = jnp.exp(jnp.where(ri >= ci, col - row, -jnp.inf))
    decay_t = jnp.exp(jnp.where(ci >= ri, row - col, -jnp.inf))
    kb = k * beta
    kbf = k.astype(BF16)
    kk = _bmm_nt(kbf, kbf)
    a = jnp.where(ri > ci, kk * decay, 0.0) * beta
    at = jnp.where(ci > ri, kk * decay_t, 0.0) * beta_row
    inv = _unit_lower_inverse(a, at, ri, ci)
    eg = jnp.exp(col)
    rhs = jnp.concatenate([kb * eg, v * beta], axis=-1).astype(BF16)
    sol = _bmm(inv.astype(BF16), rhs)
    w, u = sol[:, :, :GDN_DK].astype(BF16), sol[:, :, GDN_DK:]
    attn = (_bmm_nt(q.astype(BF16), kbf) * decay).astype(BF16)
    g_last = col[:, CHUNK - 1:CHUNK, :]
    q_dec = (q * eg).astype(BF16)
    k_dec = (k * jnp.exp(g_last - col)).astype(BF16)
    e_last = jnp.exp(g_last)
    st = state_ref[...]
    for c in range(nchunk):
        sl = slice(c * GDN_HEADS, (c + 1) * GDN_HEADS)
        stb = st.astype(BF16)
        v_new = u[sl] - _bmm(w[sl], stb)
        vnb = v_new.astype(BF16)
        o = _bmm(q_dec[sl], stb) + _bmm(attn[sl], vnb)
        st = st * e_last[sl] + jnp.einsum('hik,hiv->hkv', k_dec[sl], vnb, preferred_element_type=F32)
        o = o * lax.rsqrt(_lane_sum(o * o) * (1.0 / GDN_DV) + NORM_EPS)
        for hd in range(GDN_HEADS):
            gate = z_ref[c * CHUNK:(c + 1) * CHUNK, hd * GDN_DV:(hd + 1) * GDN_DV].astype(F32)
            o_ref[c * CHUNK:(c + 1) * CHUNK, hd * GDN_DV:(hd + 1) * GDN_DV] = (
                o[hd] * nw_ref[...] * gate).astype(BF16)
    state_ref[...] = st


def _pad_lanes(v, start):
    return jnp.zeros((SMALL_COLS,), F32).at[start:start + v.shape[0]].set(v)


def _chunk_tri(ts):
    return jnp.kron(jnp.eye(ts // CHUNK, dtype=F32), jnp.tril(jnp.ones((CHUNK, CHUNK), F32))).astype(BF16)


def _gdn_mixer(ts, a_log, dt_bias, norm_w):
    prm = jnp.stack([_pad_lanes(a_log, SMALL_A), _pad_lanes(dt_bias, SMALL_A)])
    consts = [prm, norm_w.reshape(1, -1), _chunk_tri(ts)]
    cols = [(COL_QKV, 3 * GDN_QK), (COL_ZGDN, GDN_WIDTH)]
    return dict(body=_gdn_kernel, cols=cols, small=True, consts=consts, width=GDN_WIDTH,
                scratch=pltpu.VMEM((GDN_HEADS, GDN_DK, GDN_DV), F32))


def _ssd_kernel(x_ref, z_ref, bc_ref, small_ref, prm_ref, hv_ref, tri_ref, expand_ref, smask_ref, o_ref,
                state_ref):
    x = x_ref[...].astype(F32)
    ts = x.shape[0]
    bmb, cmb = bc_ref[:, :SSD_BC], bc_ref[:, SSD_BC:]
    cm = cmb.astype(F32)
    sm = small_ref[...]
    dt_all = _softplus(sm + prm_ref[1:2, :])
    la_all = dt_all * (-jnp.exp(prm_ref[0:1, :]))
    cs = _select_rows(tri_ref[...], la_all)
    cst = cs.T
    expand = expand_ref[...]
    dtx = _select_cols(dt_all, expand)
    csx = _select_cols(cs, expand)
    xdt = x * dtx
    xdtb = xdt.astype(BF16)
    ri = lax.broadcasted_iota(jnp.int32, (CHUNK, CHUNK), 0)
    ci = lax.broadcasted_iota(jnp.int32, (CHUNK, CHUNK), 1)
    lane = lax.broadcasted_iota(jnp.int32, (CHUNK, SSD_BC), 1)
    within, updates, lasts = [], [], []
    for c in range(ts // CHUNK):
        rs = slice(c * CHUNK, (c + 1) * CHUNK)
        cb = []
        for g in range(SSD_GROUPS):
            in_g = (lane >= g * SSD_STATE) & (lane < (g + 1) * SSD_STATE)
            cb.append(_dot_nt(jnp.where(in_g, cm[rs, :], 0.0).astype(BF16), bmb[rs, :]))
        halves = []
        for pair in range(SSD_HEADS // 2):
            rhs = xdtb[rs, pair * LANES:(pair + 1) * LANES]
            ys = []
            for hd in (2 * pair, 2 * pair + 1):
                col = cs[rs, SMALL_DT + hd:SMALL_DT + hd + 1]
                row = cst[SMALL_DT + hd:SMALL_DT + hd + 1, rs]
                decay = jnp.exp(jnp.where(ri >= ci, col - row, -jnp.inf))
                ys.append(_dot((cb[hd // SSD_HPG] * decay).astype(BF16), rhs))
            halves.append(jnp.where(lane < SSD_HEAD_DIM, ys[0], ys[1]))
        within.append(jnp.concatenate(halves, axis=-1))
        cs_last_x = csx[(c + 1) * CHUNK - 1:(c + 1) * CHUNK, :]
        lasts.append(cs_last_x)
        to_end = jnp.exp(cs_last_x - csx[rs, :])
        updates.append(_dot_tn(bmb[rs, :], (xdt[rs, :] * to_end).astype(BF16)) * smask_ref[...])
    st = state_ref[...]
    ecs = jnp.exp(csx)
    parts = []
    for c in range(ts // CHUNK):
        rs = slice(c * CHUNK, (c + 1) * CHUNK)
        parts.append(within[c] + _dot(cmb[rs, :], st.astype(BF16)) * ecs[rs, :])
        st = st * jnp.exp(lasts[c]) + updates[c]
    state_ref[...] = st
    y = jnp.concatenate(parts, axis=0) + hv_ref[0:1, :] * x
    y = y * z_ref[...].astype(F32)
    y = y * lax.rsqrt(jnp.mean(y * y, axis=-1, keepdims=True) + NORM_EPS)
    o_ref[...] = (y * hv_ref[1:2, :]).astype(BF16)


def _ssd_mixer(ts, a_log, dt_bias, d, norm_w):
    prm = jnp.stack([_pad_lanes(a_log, SMALL_DT), _pad_lanes(dt_bias, SMALL_DT)])
    hv = jnp.stack([jnp.repeat(d, SSD_HEAD_DIM), norm_w])
    lane_head = jnp.arange(SSD_WIDTH) // SSD_HEAD_DIM
    expand = (jnp.arange(SMALL_COLS)[:, None] == SMALL_DT + lane_head[None, :]).astype(BF16)
    row_group = jnp.arange(SSD_BC) // SSD_STATE
    smask = (row_group[:, None] == (lane_head // SSD_HPG)[None, :]).astype(F32)
    consts = [prm, hv, _chunk_tri(ts), expand, smask]
    cols = [(COL_XSSD, SSD_WIDTH), (COL_ZSSD, SSD_WIDTH), (COL_BC, 2 * SSD_BC)]
    return dict(body=_ssd_kernel, cols=cols, small=True, consts=consts, width=SSD_WIDTH,
                scratch=pltpu.VMEM((SSD_BC, SSD_WIDTH), F32))


def _lru_kernel(x_ref, g_ref, wr_ref, wi_ref, vec_ref, o_ref, h_ref):
    step = pl.program_id(1)

    xb = x_ref[...]
    x = xb.astype(F32)
    ts = x.shape[0]
    r = _sigmoid(_dot(xb, wr_ref[...]) + vec_ref[0:1, :])
    gate_i = _sigmoid(_dot(xb, wi_ref[...]) + vec_ref[1:2, :])
    log_a = -LRU_C * r * _softplus(-vec_ref[2:3, :])
    a = jnp.exp(log_a)
    one_minus_a2 = -jnp.tanh(log_a) * (a * a + 1.0)
    mult = one_minus_a2 * lax.rsqrt(jnp.maximum(one_minus_a2, 1e-30))
    row = lax.broadcasted_iota(jnp.int32, (ts, LRU_WIDTH), 0)
    mult = jnp.where(row + step * ts == 0, 1.0, mult)
    b = mult * gate_i * x

    row_in = row % SUBLANES
    shift = 1
    while shift < SUBLANES:
        keep = row_in >= shift
        a_prev = jnp.where(keep, pltpu.roll(a, shift, 0), 1.0)
        b_prev = jnp.where(keep, pltpu.roll(b, shift, 0), 0.0)
        b = a * b_prev + b
        a = a * a_prev
        shift *= 2
    carry = h_ref[...]
    blocks = []
    for blk in range(ts // SUBLANES):
        rows = slice(blk * SUBLANES, (blk + 1) * SUBLANES)
        blocks.append(b[rows, :] + a[rows, :] * carry)
        carry = blocks[-1][SUBLANES - 1:SUBLANES, :]
    h_ref[...] = carry
    o_ref[...] = (jnp.concatenate(blocks, axis=0) * g_ref[...].astype(F32)).astype(BF16)


def _lru_mixer(lam, wr, br, wi, bi):
    eye = jnp.eye(LRU_BLOCKS, dtype=F32)

    def blockdiag(w):
        return jnp.einsum('nde,nm->ndme', w, eye).reshape(LRU_WIDTH, LRU_WIDTH).astype(BF16)

    consts = [blockdiag(wr), blockdiag(wi), jnp.stack([br, bi, lam])]
    cols = [(COL_XLRU, LRU_WIDTH), (COL_GLRU, LRU_WIDTH)]
    return dict(body=_lru_kernel, cols=cols, small=False, consts=consts, width=LRU_WIDTH,
                scratch=pltpu.VMEM((1, LRU_WIDTH), F32))


def _mixers_kernel(*refs, layout):
    n_in = sum(n for _, n in layout)
    nmix = len(layout)
    gm_ref, x_ref, mod_ref = refs[n_in:n_in + 3]
    w_refs = refs[n_in + 3:n_in + 3 + nmix]
    wo_ref, o_ref = refs[n_in + 3 + nmix:n_in + 5 + nmix]
    states = refs[n_in + 5 + nmix:n_in + 5 + 2 * nmix]
    ybufs = refs[n_in + 5 + 2 * nmix:]

    @pl.when(pl.program_id(1) == 0)
    def _():
        for st in states:
            st[...] = jnp.zeros_like(st)

    pos = 0
    merged = None
    for i, ((body, n), ybuf, st, w_ref) in enumerate(zip(layout, ybufs, states, w_refs)):
        body(*refs[pos:pos + n], ybuf, st)
        pos += n
        term = gm_ref[:, i * D_MODEL:(i + 1) * D_MODEL].astype(F32) * _dot(ybuf[...], w_ref[...])
        merged = term if merged is None else merged + term
    out = _dot(merged.astype(BF16), wo_ref[...])
    o_ref[...] = x_ref[...] + mod_ref[:, 2 * D_MODEL:3 * D_MODEL] * out


def _mixers_merge(p_main, small, x2, mod3, w_branch, w_out, bsz, seq, parts):
    ts = min(MIX_TILE, seq)
    nsb = seq // ts

    def rows(b, s):
        return b * nsb + s

    operands, in_specs, layout = [], [], []
    for part in parts:
        for off, width in part["cols"]:
            operands.append(p_main)
            in_specs.append(pl.BlockSpec((ts, width), functools.partial(
                lambda b, s, col: (rows(b, s), col), col=off // width)))
        if part["small"]:
            operands.append(small)
            in_specs.append(pl.BlockSpec((ts, SMALL_COLS), lambda b, s: (rows(b, s), 0)))
        operands += part["consts"]
        in_specs += [_full(a.shape) for a in part["consts"]]
        layout.append((part["body"], len(part["cols"]) + int(part["small"]) + len(part["consts"])))
    gw = N_BRANCH * D_MODEL
    operands += [p_main, x2, mod3]
    in_specs += [
        pl.BlockSpec((ts, gw), lambda b, s: (rows(b, s), COL_GMERGE // gw)),
        pl.BlockSpec((ts, D_MODEL), lambda b, s: (rows(b, s), 0)),
        pl.BlockSpec((None, 1, 6 * D_MODEL), lambda b, s: (b, 0, 0)),
    ]
    acc = 0
    for part in parts:
        operands.append(w_branch[acc:acc + part["width"]].astype(BF16))
        in_specs.append(_full((part["width"], D_MODEL)))
        acc += part["width"]
    operands.append(w_out.astype(BF16))
    in_specs.append(_full((D_MODEL, D_MODEL)))
    return pl.pallas_call(
        functools.partial(_mixers_kernel, layout=tuple(layout)),
        grid=(bsz, nsb),
        in_specs=in_specs,
        out_specs=pl.BlockSpec((ts, D_MODEL), lambda b, s: (rows(b, s), 0)),
        out_shape=jax.ShapeDtypeStruct((bsz * seq, D_MODEL), F32),
        scratch_shapes=[part["scratch"] for part in parts]
        + [pltpu.VMEM((ts, part["width"]), BF16) for part in parts],
        compiler_params=_params("parallel", "arbitrary"),
        name="mixers_merge",
    )(*operands)


FFN_SLICES = tuple((s, min(512, FFN_HIDDEN - s)) for s in range(0, FFN_HIDDEN, 512))


def _ffn_kernel(x_ref, mod_ref, g_ref, w13_ref, w2_ref, gf_ref, o_ref, *, final):
    x = x_ref[...]
    h = _modulated_norm(x, g_ref[...], mod_ref[:, 3 * D_MODEL:4 * D_MODEL], mod_ref[:, 4 * D_MODEL:5 * D_MODEL])
    hb = h.astype(BF16)
    acc = None
    for start, size in FFN_SLICES:
        a = _dot(hb, w13_ref[:, start:start + size])
        b = _dot(hb, w13_ref[:, FFN_HIDDEN + start:FFN_HIDDEN + start + size])
        part = _dot((_silu(a) * b).astype(BF16), w2_ref[start:start + size, :])
        acc = part if acc is None else acc + part
    x = x + mod_ref[:, 5 * D_MODEL:6 * D_MODEL] * acc
    if final:
        x = x * lax.rsqrt(jnp.mean(x * x, axis=-1, keepdims=True) + NORM_EPS) * gf_ref[...]
    o_ref[...] = x


def _ffn(x2, mod3, ln_g, w13, w2, ln_final, seq, final):
    t = x2.shape[0]
    tm = min(512, seq)
    per_b = seq // tm
    return pl.pallas_call(
        functools.partial(_ffn_kernel, final=final),
        grid=(t // tm,),
        in_specs=[
            pl.BlockSpec((tm, D_MODEL), lambda i: (i, 0)),
            pl.BlockSpec((None, 1, 6 * D_MODEL), lambda i: (i // per_b, 0, 0)),
            _full((1, D_MODEL)),
            pl.BlockSpec((D_MODEL, 2 * FFN_HIDDEN), lambda i: (0, 0), pipeline_mode=pl.Buffered(1)),
            pl.BlockSpec((FFN_HIDDEN, D_MODEL), lambda i: (0, 0), pipeline_mode=pl.Buffered(1)),
            _full((1, D_MODEL)),
        ],
        out_specs=pl.BlockSpec((tm, D_MODEL), lambda i: (i, 0)),
        out_shape=jax.ShapeDtypeStruct((t, D_MODEL), F32),
        compiler_params=_params("parallel"),
        name="ffn_final" if final else "ffn",
    )(x2, mod3, ln_g, w13, w2, ln_final)


def _split_in_weights(w_in, conv_w, conv_b):
    o = 0
    seg = {}
    for name, width in (("q", GDN_QK), ("k", GDN_QK), ("v", GDN_WIDTH), ("xssd", SSD_WIDTH), ("bssd", SSD_BC),
                        ("cssd", SSD_BC), ("xlru", LRU_WIDTH), ("us5", S5_WIDTH), ("bgdn", GDN_HEADS),
                        ("agdn", GDN_HEADS), ("zgdn", GDN_WIDTH), ("zssd", SSD_WIDTH), ("dtssd", SSD_HEADS),
                        ("glru", LRU_WIDTH), ("gmerge", N_BRANCH * D_MODEL)):
        seg[name] = (o, o + width)
        o += width

    def cols(arr, names):
        return jnp.concatenate([arr[:, seg[n][0]:seg[n][1]] for n in names], axis=1)

    conv_names = ("q", "k", "v", "xssd", "xlru", "bssd", "cssd")
    wb = w_in.astype(BF16)
    us5 = wb[:, seg["us5"][0]:seg["us5"][1]]
    main = jnp.concatenate([cols(wb, conv_names), us5[:, :2 * LANES], cols(wb, ("zgdn", "zssd", "gmerge", "glru")),
                            us5[:, 2 * LANES:]], axis=1)
    small_names = ("bgdn", "agdn", "dtssd")
    used = sum(seg[n][1] - seg[n][0] for n in small_names)
    small = jnp.concatenate([cols(wb, small_names), jnp.zeros((w_in.shape[0], SMALL_COLS - used), BF16)], axis=1)
    return main, small, cols(conv_w, conv_names), cols(conv_b.reshape(1, -1), conv_names)


def kernel(x, c, ln_mix_g, ln_ffn_g, ln_final_g, ada_w, ada_b, w_in, conv_w, conv_b, s5_lambda_re, s5_lambda_im, s5_log_dt, s5_b_re, s5_b_im, s5_c_re, s5_c_im, s5_d, s5_glu_w, s5_glu_b, gdn_a_log, gdn_dt_bias, gdn_norm_w, ssd_a_log, ssd_dt_bias, ssd_d, ssd_norm_w, lru_lambda, lru_wr, lru_br, lru_wi, lru_bi, w_branch, w_out, ffn_w13, ffn_w2):
    bsz, seq, d = x.shape
    t = bsz * seq
    x2 = x.reshape(t, d)
    mod = _ada_mod(c, ada_w, ada_b)
    ln_final = ln_final_g.reshape(1, d)
    for l in range(DEPTH):
        mod3 = mod[l].reshape(bsz, 1, 6 * d)
        w_main, w_small, cw, cb = _split_in_weights(w_in[l], conv_w[l], conv_b[l])
        p_main, small = _inproj(x2, mod3, ln_mix_g[l].reshape(1, d), w_main, w_small, cw, cb, seq)

        tables = _s5_tables(s5_lambda_re[l], s5_lambda_im[l], s5_log_dt[l], s5_b_re[l], s5_b_im[l],
                            s5_c_re[l], s5_c_im[l], min(CHUNK, seq))
        mix_ts = min(MIX_TILE, seq)
        x2 = _mixers_merge(p_main, small, x2, mod3, w_branch[l], w_out[l], bsz, seq, (
            _s5_mixer(tables, s5_d[l], s5_glu_w[l], s5_glu_b[l]),
            _gdn_mixer(mix_ts, gdn_a_log[l], gdn_dt_bias[l], gdn_norm_w[l]),
            _ssd_mixer(mix_ts, ssd_a_log[l], ssd_dt_bias[l], ssd_d[l], ssd_norm_w[l]),
            _lru_mixer(lru_lambda[l], lru_wr[l], lru_br[l], lru_wi[l], lru_bi[l])))
        x2 = _ffn(x2, mod3, ln_ffn_g[l].reshape(1, d), ffn_w13[l].astype(BF16), ffn_w2[l].astype(BF16),
                  ln_final, seq, final=(l == DEPTH - 1))
    return x2.reshape(bsz, seq, d)
```

```python
import functools

import jax
import jax.numpy as jnp
from jax import lax
from jax.experimental import pallas as pl
from jax.experimental.pallas import tpu as pltpu

F32 = jnp.float32
BF16 = jnp.bfloat16

D_MODEL = 1024
DEPTH = 2
CONV_K = 4
NORM_EPS = 1e-6
S5_GROUP, S5_STATE, S5_WIDTH, S5_GROUPS = 16, 64, 384, 24
GDN_HEADS, GDN_DK, GDN_DV = 4, 128, 128
GDN_QK = GDN_HEADS * GDN_DK
GDN_WIDTH = GDN_HEADS * GDN_DV
SSD_HEAD_DIM, SSD_WIDTH, SSD_HEADS, SSD_GROUPS, SSD_HPG, SSD_STATE = 64, 512, 8, 2, 4, 64
SSD_BC = SSD_GROUPS * SSD_STATE
LRU_WIDTH, LRU_BLOCK, LRU_BLOCKS, LRU_C = 512, 64, 8, 8.0
N_BRANCH = 4
FFN_HIDDEN = 2816

LANES = 128
SUBLANES = 8

COL_QKV, COL_XSSD, COL_XLRU, COL_BC = 0, 1536, 2048, 2560
CONV_COLS = 2816
SILU_CONV_END = 2048
COL_US5A, COL_ZGDN, COL_ZSSD, COL_GMERGE, COL_GLRU, COL_US5B = 2816, 3072, 3584, 4096, 8192, 8704
MAIN_COLS = 8832
PROJ_TILES = 3
PROJ_TN = MAIN_COLS // PROJ_TILES
PROJ_ROW_CHUNKS = 4
SMALL_COLS = 128
SMALL_B, SMALL_A, SMALL_DT = 0, 4, 8

CHUNK = 128
MIX_TILE = 256
S5_PARTS = S5_WIDTH // LANES
S5_PART_STATE = (S5_GROUPS // S5_PARTS) * S5_STATE
VMEM_LIMIT = 56 * 1024 * 1024


def _sigmoid(x):
    return 0.5 * jnp.tanh(0.5 * x) + 0.5


def _silu(x):
    return x * _sigmoid(x)


def _softplus(x):
    return jnp.maximum(x, 0.0) + jnp.log1p(jnp.exp(-jnp.abs(x)))


def _gelu(x):
    return jax.nn.gelu(x, approximate=True)


def _dot(a, b):
    return jnp.dot(a, b, preferred_element_type=F32)


def _dot_nt(a, b):
    return lax.dot_general(a, b, (((1,), (1,)), ((), ())), preferred_element_type=F32)


def _dot_tn(a, b):
    return lax.dot_general(a, b, (((0,), (0,)), ((), ())), preferred_element_type=F32)


def _bmm(a, b):
    return jnp.einsum('gij,gjk->gik', a, b, preferred_element_type=F32)


def _bmm_nt(a, b):
    return jnp.einsum('gik,gjk->gij', a, b, preferred_element_type=F32)


def _split3(x):
    x1 = x.astype(BF16)
    r1 = x - x1.astype(F32)
    x2 = r1.astype(BF16)
    x3 = (r1 - x2.astype(F32)).astype(BF16)
    return x1, x2, x3


def _select_rows(sel_bf16, x):
    x1, x2, x3 = _split3(x)
    return _dot(sel_bf16, x1) + (_dot(sel_bf16, x2) + _dot(sel_bf16, x3))


def _select_cols(x, sel_bf16):
    x1, x2, x3 = _split3(x)
    return _dot(x1, sel_bf16) + (_dot(x2, sel_bf16) + _dot(x3, sel_bf16))


def _lane_sum(x):
    shape = x.shape
    ones = jnp.ones((shape[-1], shape[-1]), BF16)
    return _dot(x.reshape(-1, shape[-1]).astype(BF16), ones).reshape(shape)


def _params(*sem):
    return pltpu.CompilerParams(dimension_semantics=sem, vmem_limit_bytes=VMEM_LIMIT)


def _full(shape):
    n = len(shape)
    return pl.BlockSpec(shape, lambda *_: (0,) * n)


def _ada_kernel(c_ref, w_ref, b_ref, o_ref):
    cond = _silu(c_ref[...]).astype(BF16)
    o_ref[...] = _dot(cond, w_ref[...].astype(BF16)) + b_ref[...]


def _ada_mod(c, ada_w, ada_b):
    depth, d, n = ada_w.shape
    bsz = c.shape[0]
    tn = 1536
    return pl.pallas_call(
        _ada_kernel,
        grid=(depth, n // tn),
        in_specs=[
            pl.BlockSpec((bsz, d), lambda l, j: (0, 0)),
            pl.BlockSpec((None, d, tn), lambda l, j: (l, 0, j)),
            pl.BlockSpec((None, 1, tn), lambda l, j: (l, 0, j)),
        ],
        out_specs=pl.BlockSpec((None, bsz, tn), lambda l, j: (l, 0, j)),
        out_shape=jax.ShapeDtypeStruct((depth, bsz, n), F32),
        compiler_params=_params("parallel", "parallel"),
        name="ada_mod",
    )(c, ada_w, ada_b.reshape(depth, 1, n))


def _modulated_norm(x, g, shift, scale):
    ms = jnp.mean(x * x, axis=-1, keepdims=True)
    return x * lax.rsqrt(ms + NORM_EPS) * g * (1.0 + scale) + shift


CONV_STRIDE = 4


def _causal_conv(x, xpad_ref, stage_ref, w_ref, b_ref, emit):
    ts, c = x.shape
    span = CONV_STRIDE * SUBLANES

    for tile in range(c // LANES):
        lanes = slice(tile * LANES, (tile + 1) * LANES)
        xt = x[:, lanes]
        xpad_ref[tile, SUBLANES:SUBLANES + ts, :] = xt
        taps = [jnp.broadcast_to(w_ref[k:k + 1, lanes], (SUBLANES, LANES)) for k in range(CONV_K)]
        bias = jnp.broadcast_to(b_ref[:, lanes], (SUBLANES, LANES))
        for group in range(ts // span):
            for res in range(CONV_STRIDE):
                row0 = group * span + res
                acc = bias
                for k in range(CONV_K):
                    start = SUBLANES + row0 - (CONV_K - 1) + k
                    acc = acc + taps[k] * xpad_ref[tile, pl.ds(start, SUBLANES, stride=CONV_STRIDE), :]
                stage_ref[tile, pl.ds(row0, SUBLANES, stride=CONV_STRIDE), :] = acc
        xpad_ref[tile, 0:SUBLANES, :] = xt[ts - SUBLANES:, :]
        emit(tile, stage_ref[tile])


def _inproj_kernel(x_ref, mod_ref, g_ref, w_ref, cw_ref, cb_ref, p_ref, small_ref, xpad_ref, stage_ref, *,
                   per_b):
    rows = x_ref.shape[0] // PROJ_ROW_CHUNKS

    @pl.when(pl.program_id(0) % per_b == 0)
    def _():
        xpad_ref[:, 0:SUBLANES, :] = jnp.zeros((CONV_COLS // LANES, SUBLANES, LANES), F32)

    def column_tile_0(r, put):
        def emit(tile, y):
            lo = tile * LANES
            plain = SILU_CONV_END <= lo < COL_BC
            put(lo, lo + LANES, y if plain else _silu(y))

        _causal_conv(r[:, :CONV_COLS], xpad_ref, stage_ref, cw_ref, cb_ref, emit)
        put(CONV_COLS, PROJ_TN, r[:, CONV_COLS:])

    def column_tile_1(r, put):
        z0, g0 = COL_ZGDN - PROJ_TN, COL_GMERGE - PROJ_TN
        put(PROJ_TN, PROJ_TN + z0, r[:, :z0])
        put(PROJ_TN + z0, PROJ_TN + g0, _silu(r[:, z0:g0]))
        put(PROJ_TN + g0, 2 * PROJ_TN, _sigmoid(r[:, g0:]))

    def column_tile_2(r, put):
        l0, u0 = COL_GLRU - 2 * PROJ_TN, COL_US5B - 2 * PROJ_TN
        put(2 * PROJ_TN, 2 * PROJ_TN + l0, _sigmoid(r[:, :l0]))
        put(2 * PROJ_TN + l0, 2 * PROJ_TN + u0, _gelu(r[:, l0:u0]))
        put(2 * PROJ_TN + u0, 3 * PROJ_TN, r[:, u0:])

    for rc in range(PROJ_ROW_CHUNKS):
        r0 = rc * rows
        h = _modulated_norm(x_ref[r0:r0 + rows, :], g_ref[...], mod_ref[:, 0:D_MODEL],
                            mod_ref[:, D_MODEL:2 * D_MODEL])
        hb = h.astype(BF16)

        def put(lo, hi, val, r0=r0):
            p_ref[r0:r0 + rows, lo:hi] = val.astype(BF16)

        for jt, epilogue in enumerate((column_tile_0, column_tile_1, column_tile_2)):
            last = jt == PROJ_TILES - 1
            r = _dot(hb, w_ref[:, jt * PROJ_TN:(jt + 1) * PROJ_TN + (SMALL_COLS if last else 0)])
            if last:
                small_ref[r0:r0 + rows, :] = r[:, PROJ_TN:]
                r = r[:, :PROJ_TN]
            epilogue(r, put)


def _inproj(x2, mod3, ln_g, w_all, conv_w, conv_b, seq, layer):
    t = x2.shape[0]
    tm = min(512, seq)
    per_b = seq // tm
    return pl.pallas_call(
        functools.partial(_inproj_kernel, per_b=per_b),
        grid=(t // tm,),
        in_specs=[
            pl.BlockSpec((tm, D_MODEL), lambda i: (i, 0)),
            pl.BlockSpec((None, 1, 6 * D_MODEL), lambda i: (i // per_b, 0, 0)),
            pl.BlockSpec((1, D_MODEL), lambda i: (0, 0)),
            pl.BlockSpec((None, D_MODEL, MAIN_COLS + SMALL_COLS), lambda i: (layer, 0, 0),
                         pipeline_mode=pl.Buffered(1)),
            pl.BlockSpec((None, CONV_K, CONV_COLS), lambda i: (layer, 0, 0)),
            pl.BlockSpec((None, 1, CONV_COLS), lambda i: (layer, 0, 0)),
        ],
        out_specs=[
            pl.BlockSpec((tm, MAIN_COLS), lambda i: (i, 0)),
            pl.BlockSpec((tm, SMALL_COLS), lambda i: (i, 0)),
        ],
        out_shape=[
            jax.ShapeDtypeStruct((t, MAIN_COLS), BF16),
            jax.ShapeDtypeStruct((t, SMALL_COLS), F32),
        ],
        scratch_shapes=[pltpu.VMEM((CONV_COLS // LANES, tm // PROJ_ROW_CHUNKS + SUBLANES, LANES), F32),
                        pltpu.VMEM((CONV_COLS // LANES, tm // PROJ_ROW_CHUNKS, LANES), F32)],
        compiler_params=_params("arbitrary"),
        name="inproj",
    )(x2, mod3, ln_g, w_all, conv_w, conv_b)


def _s5_kernel(u0_ref, u1_ref, u2_ref, wb_ref, pneg_ref, ppos_ref, lam1_ref, tri_ref, wc_ref, d_ref, gw_ref,
               gb_ref, o_ref, carry_ref):
    h = S5_PART_STATE
    nchunk = u0_ref.shape[0] // CHUNK
    tri = tri_ref[...]
    ys = [[None] * S5_PARTS for _ in range(nchunk)]
    for part, u_ref in enumerate((u0_ref, u1_ref, u2_ref)):
        lo = part * 2 * h
        ub = u_ref[...]
        bu = _dot(ub, wb_ref[part])
        nr, ni = pneg_ref[:, lo:lo + h], pneg_ref[:, lo + h:lo + 2 * h]
        sums = []
        for c in range(nchunk):
            br, bi = bu[c * CHUNK:(c + 1) * CHUNK, :h], bu[c * CHUNK:(c + 1) * CHUNK, h:]
            z = jnp.concatenate([nr * br - ni * bi, nr * bi + ni * br], axis=-1).astype(BF16)
            sums.append(_dot(tri, z))
        cr, ci = carry_ref[:, lo:lo + h], carry_ref[:, lo + h:lo + 2 * h]
        l1r, l1i = lam1_ref[:, lo:lo + h], lam1_ref[:, lo + h:lo + 2 * h]
        pr, pi = ppos_ref[:, lo:lo + h], ppos_ref[:, lo + h:lo + 2 * h]
        for c in range(nchunk):
            s = sums[c]
            sr = s[:, :h] + (l1r * cr - l1i * ci)
            si = s[:, h:] + (l1r * ci + l1i * cr)
            xr = pr * sr - pi * si
            xi = pr * si + pi * sr
            cr, ci = xr[CHUNK - 1:CHUNK, :], xi[CHUNK - 1:CHUNK, :]
            xcat = jnp.concatenate([xr, xi], axis=-1).astype(BF16)
            skip = d_ref[:, part * LANES:(part + 1) * LANES] * ub[c * CHUNK:(c + 1) * CHUNK, :].astype(F32)
            ys[c][part] = _dot(xcat, wc_ref[part]) + skip
        carry_ref[:, lo:lo + h] = cr
        carry_ref[:, lo + h:lo + 2 * h] = ci
    y = _gelu(jnp.concatenate([jnp.concatenate(row, axis=-1) for row in ys], axis=0))
    o_ref[...] = (y * _sigmoid(_dot(y.astype(BF16), gw_ref[...]) + gb_ref[...])).astype(BF16)


def _s5_tables(lam_re, lam_im, log_dt, b_re, b_im, c_re, c_im, ts):
    gp = S5_GROUPS // S5_PARTS
    dt = jnp.exp(log_dt)[:, None]
    ar, ai = lam_re * dt, lam_im * dt
    er = jnp.exp(ar)
    lbr, lbi = er * jnp.cos(ai), er * jnp.sin(ai)
    den = lam_re * lam_re + lam_im * lam_im
    qr = ((lbr - 1.0) * lam_re + lbi * lam_im) / den
    qi = (lbi * lam_re - (lbr - 1.0) * lam_im) / den
    bbr = qr[..., None] * b_re - qi[..., None] * b_im
    bbi = qr[..., None] * b_im + qi[..., None] * b_re
    eye = jnp.eye(gp, dtype=F32)

    def blockdiag_in(m):
        m = m.reshape(S5_PARTS, gp, S5_STATE, S5_GROUP)
        return jnp.einsum('qgph,gk->qghkp', m, eye).reshape(S5_PARTS, LANES, S5_PART_STATE)

    def blockdiag_out(m):
        m = m.reshape(S5_PARTS, gp, S5_GROUP, S5_STATE)
        return jnp.einsum('qghp,gk->qgpkh', m, eye).reshape(S5_PARTS, S5_PART_STATE, LANES)

    wb = jnp.concatenate([blockdiag_in(bbr), blockdiag_in(bbi)], axis=2).astype(BF16)
    wc = jnp.concatenate([blockdiag_out(c_re), -blockdiag_out(c_im)], axis=1).astype(BF16)

    def lanes(re, im):
        lead = re.shape[:-2]
        re = re.reshape(lead + (S5_PARTS, 1, S5_PART_STATE))
        im = im.reshape(lead + (S5_PARTS, 1, S5_PART_STATE))
        return jnp.concatenate([re, im], axis=-2).reshape(lead + (S5_PARTS * 2 * S5_PART_STATE,))

    j = jnp.arange(ts, dtype=F32)[:, None, None]

    def power(sign):
        mag = jnp.exp(sign * j * ar[None])
        ang = sign * j * ai[None]
        return lanes(mag * jnp.cos(ang), mag * jnp.sin(ang))

    return wb, power(-1.0), power(1.0), lanes(lbr, lbi)[None, :], wc


def _s5_mixer(tables, d, glu_w, glu_b):
    wb, pneg, ppos, lam1, wc = tables
    tri = jnp.tril(jnp.ones((CHUNK, CHUNK), F32)).astype(BF16)
    consts = [wb, pneg, ppos, lam1, tri, wc, d.reshape(1, -1), glu_w.astype(BF16), glu_b.reshape(1, -1)]
    cols = [(COL_US5A, LANES), (COL_US5A + LANES, LANES), (COL_US5B, LANES)]
    return dict(body=_s5_kernel, cols=cols, small=False, consts=consts, width=S5_WIDTH,
                scratch=pltpu.VMEM((1, S5_PARTS * 2 * S5_PART_STATE), F32))


INV_BASE = 16


def _block_inverse_rows(mt):
    groups, n, w = mt.shape
    mt2 = mt.reshape(groups * n, w)
    sub = lax.broadcasted_iota(jnp.int32, (groups * n, w), 0) % n
    lane = lax.broadcasted_iota(jnp.int32, (groups * n, w), 1)
    lane_in = lane % n
    lane_base = lane - lane_in
    x = (sub == lane_in).astype(F32)
    for i in range(1, n):
        coef = jnp.take_along_axis(mt2, lane_base + i, axis=1)
        acc = jnp.sum((coef * x).reshape(groups, n, w), axis=1, keepdims=True)
        new_row = jnp.where(lane_in[0:1, :] == i, 1.0, 0.0) - acc
        new_rows = jnp.broadcast_to(new_row, (groups, n, w)).reshape(groups * n, w)
        x = jnp.where(sub == i, new_rows, x)
    return x.reshape(groups, n, w)


def _unit_lower_inverse(a, at, ri, ci):
    n = a.shape[1]
    assert n == LANES
    diag_blk = ri // INV_BASE == ci // INV_BASE
    at_blk = jnp.where(diag_blk, at, 0.0)
    mt = at_blk[:, 0:INV_BASE, :]
    for blk in range(1, n // INV_BASE):
        mt = mt + at_blk[:, blk * INV_BASE:(blk + 1) * INV_BASE, :]
    x = _block_inverse_rows(mt)
    inv = jnp.where(diag_blk, jnp.concatenate([x] * (n // INV_BASE), axis=1), 0.0)
    size = INV_BASE
    while size < n:
        lower_left = (ri // (2 * size) == ci // (2 * size)) & ((ri // size) % 2 == 1) & ((ci // size) % 2 == 0)
        invb = inv.astype(BF16)
        inv = inv - _bmm(_bmm(invb, jnp.where(lower_left, a, 0.0).astype(BF16)).astype(BF16), invb)
        size *= 2
    return inv


def _gdn_kernel(qkv_ref, z_ref, small_ref, prm_ref, nw_ref, tri_ref, o_ref, state_ref):
    ts = qkv_ref.shape[0]
    sm = small_ref[...]
    beta_all = _sigmoid(sm)
    g_all = -jnp.exp(prm_ref[0:1, :]) * _softplus(sm + prm_ref[1:2, :])
    gc = _select_rows(tri_ref[...], g_all)
    small_lane = lax.broadcasted_iota(jnp.int32, sm.shape, 1)
    rows_t = jnp.where(small_lane < SMALL_A, beta_all, gc).T
    nchunk = ts // CHUNK
    ri = lax.broadcasted_iota(jnp.int32, (CHUNK, CHUNK), 0)
    ci = lax.broadcasted_iota(jnp.int32, (CHUNK, CHUNK), 1)
    groups = [(c, hd) for c in range(nchunk) for hd in range(GDN_HEADS)]

    def head_tiles(off):
        return jnp.stack([qkv_ref[c * CHUNK:(c + 1) * CHUNK, off + hd * GDN_DK:off + (hd + 1) * GDN_DK]
                          for c, hd in groups]).astype(F32)

    def head_cols(arr, lane0):
        return jnp.stack([arr[c * CHUNK:(c + 1) * CHUNK, lane0 + hd:lane0 + hd + 1] for c, hd in groups])

    def head_rows(lane0):
        return jnp.stack([rows_t[lane0 + hd:lane0 + hd + 1, c * CHUNK:(c + 1) * CHUNK] for c, hd in groups])

    q = head_tiles(0)
    k = head_tiles(GDN_QK)
    v = head_tiles(2 * GDN_QK)
    q = q * lax.rsqrt(_lane_sum(q * q) + NORM_EPS) * (GDN_DK ** -0.5)
    k = k * lax.rsqrt(_lane_sum(k * k) + NORM_EPS)
    wide = (len(groups), CHUNK, LANES)
    col, row = jnp.broadcast_to(head_cols(gc, SMALL_A), wide), head_rows(SMALL_A)
    beta, beta_row = jnp.broadcast_to(head_cols(beta_all, SMALL_B), wide), head_rows(SMALL_B)
    decay = jnp.exp(jnp.where(ri >= ci, col - row, -jnp.inf))
    decay_t = jnp.exp(jnp.where(ci >= ri, row - col, -jnp.inf))
    kb = k * beta
    kbf = k.astype(BF16)
    kk = _bmm_nt(kbf, kbf)
    a = jnp.where(ri > ci, kk * decay, 0.0) * beta
    at = jnp.where(ci > ri, kk * decay_t, 0.0) * beta_row
    inv = _unit_lower_inverse(a, at, ri, ci)
    eg = jnp.exp(col)
    rhs = jnp.concatenate([kb * eg, v * beta], axis=-1).astype(BF16)
    sol = _bmm(inv.astype(BF16), rhs)
    w, u = sol[:, :, :GDN_DK].astype(BF16), sol[:, :, GDN_DK:]
    attn = (_bmm_nt(q.astype(BF16), kbf) * decay).astype(BF16)
    g_last = col[:, CHUNK - 1:CHUNK, :]
    q_dec = (q * eg).astype(BF16)
    k_dec = (k * jnp.exp(g_last - col)).astype(BF16)
    e_last = jnp.exp(g_last)
    st = state_ref[...]
    for c in range(nchunk):
        sl = slice(c * GDN_HEADS, (c + 1) * GDN_HEADS)
        stb = st.astype(BF16)
        v_new = u[sl] - _bmm(w[sl], stb)
        vnb = v_new.astype(BF16)
        o = _bmm(q_dec[sl], stb) + _bmm(attn[sl], vnb)
        st = st * e_last[sl] + jnp.einsum('hik,hiv->hkv', k_dec[sl], vnb, preferred_element_type=F32)
        o = o * lax.rsqrt(_lane_sum(o * o) * (1.0 / GDN_DV) + NORM_EPS)
        for hd in range(GDN_HEADS):
            gate = z_ref[c * CHUNK:(c + 1) * CHUNK, hd * GDN_DV:(hd + 1) * GDN_DV].astype(F32)
            o_ref[c * CHUNK:(c + 1) * CHUNK, hd * GDN_DV:(hd + 1) * GDN_DV] = (
                o[hd] * nw_ref[...] * gate).astype(BF16)
    state_ref[...] = st


def _pad_lanes(v, start):
    return jnp.zeros((SMALL_COLS,), F32).at[start:start + v.shape[0]].set(v)


def _chunk_tri(ts):
    return jnp.kron(jnp.eye(ts // CHUNK, dtype=F32), jnp.tril(jnp.ones((CHUNK, CHUNK), F32))).astype(BF16)


def _gdn_mixer(ts, a_log, dt_bias, norm_w):
    prm = jnp.stack([_pad_lanes(a_log, SMALL_A), _pad_lanes(dt_bias, SMALL_A)])
    consts = [prm, norm_w.reshape(1, -1), _chunk_tri(ts)]
    cols = [(COL_QKV, 3 * GDN_QK), (COL_ZGDN, GDN_WIDTH)]
    return dict(body=_gdn_kernel, cols=cols, small=True, consts=consts, width=GDN_WIDTH,
                scratch=pltpu.VMEM((GDN_HEADS, GDN_DK, GDN_DV), F32))


def _ssd_kernel(x_ref, z_ref, bc_ref, small_ref, prm_ref, hv_ref, tri_ref, expand_ref, smask_ref, o_ref,
                state_ref):
    x = x_ref[...].astype(F32)
    ts = x.shape[0]
    bmb, cmb = bc_ref[:, :SSD_BC], bc_ref[:, SSD_BC:]
    cm = cmb.astype(F32)
    sm = small_ref[...]
    dt_all = _softplus(sm + prm_ref[1:2, :])
    la_all = dt_all * (-jnp.exp(prm_ref[0:1, :]))
    cs = _select_rows(tri_ref[...], la_all)
    cst = cs.T
    expand = expand_ref[...]
    dtx = _select_cols(dt_all, expand)
    csx = _select_cols(cs, expand)
    xdt = x * dtx
    xdtb = xdt.astype(BF16)
    ri = lax.broadcasted_iota(jnp.int32, (CHUNK, CHUNK), 0)
    ci = lax.broadcasted_iota(jnp.int32, (CHUNK, CHUNK), 1)
    lane = lax.broadcasted_iota(jnp.int32, (CHUNK, SSD_BC), 1)
    within, updates, lasts = [], [], []
    for c in range(ts // CHUNK):
        rs = slice(c * CHUNK, (c + 1) * CHUNK)
        cb = []
        for g in range(SSD_GROUPS):
            in_g = (lane >= g * SSD_STATE) & (lane < (g + 1) * SSD_STATE)
            cb.append(_dot_nt(jnp.where(in_g, cm[rs, :], 0.0).astype(BF16), bmb[rs, :]))
        halves = []
        for pair in range(SSD_HEADS // 2):
            rhs = xdtb[rs, pair * LANES:(pair + 1) * LANES]
            ys = []
            for hd in (2 * pair, 2 * pair + 1):
                col = cs[rs, SMALL_DT + hd:SMALL_DT + hd + 1]
                row = cst[SMALL_DT + hd:SMALL_DT + hd + 1, rs]
                decay = jnp.exp(jnp.where(ri >= ci, col - row, -jnp.inf))
                ys.append(_dot((cb[hd // SSD_HPG] * decay).astype(BF16), rhs))
            halves.append(jnp.where(lane < SSD_HEAD_DIM, ys[0], ys[1]))
        within.append(jnp.concatenate(halves, axis=-1))
        cs_last_x = csx[(c + 1) * CHUNK - 1:(c + 1) * CHUNK, :]
        lasts.append(cs_last_x)
        to_end = jnp.exp(cs_last_x - csx[rs, :])
        updates.append(_dot_tn(bmb[rs, :], (xdt[rs, :] * to_end).astype(BF16)) * smask_ref[...])
    st = state_ref[...]
    ecs = jnp.exp(csx)
    parts = []
    for c in range(ts // CHUNK):
        rs = slice(c * CHUNK, (c + 1) * CHUNK)
        parts.append(within[c] + _dot(cmb[rs, :], st.astype(BF16)) * ecs[rs, :])
        st = st * jnp.exp(lasts[c]) + updates[c]
    state_ref[...] = st
    y = jnp.concatenate(parts, axis=0) + hv_ref[0:1, :] * x
    y = y * z_ref[...].astype(F32)
    y = y * lax.rsqrt(jnp.mean(y * y, axis=-1, keepdims=True) + NORM_EPS)
    o_ref[...] = (y * hv_ref[1:2, :]).astype(BF16)


def _ssd_mixer(ts, a_log, dt_bias, d, norm_w):
    prm = jnp.stack([_pad_lanes(a_log, SMALL_DT), _pad_lanes(dt_bias, SMALL_DT)])
    hv = jnp.stack([jnp.repeat(d, SSD_HEAD_DIM), norm_w])
    lane_head = jnp.arange(SSD_WIDTH) // SSD_HEAD_DIM
    expand = (jnp.arange(SMALL_COLS)[:, None] == SMALL_DT + lane_head[None, :]).astype(BF16)
    row_group = jnp.arange(SSD_BC) // SSD_STATE
    smask = (row_group[:, None] == (lane_head // SSD_HPG)[None, :]).astype(F32)
    consts = [prm, hv, _chunk_tri(ts), expand, smask]
    cols = [(COL_XSSD, SSD_WIDTH), (COL_ZSSD, SSD_WIDTH), (COL_BC, 2 * SSD_BC)]
    return dict(body=_ssd_kernel, cols=cols, small=True, consts=consts, width=SSD_WIDTH,
                scratch=pltpu.VMEM((SSD_BC, SSD_WIDTH), F32))


def _lru_kernel(x_ref, g_ref, wr_ref, wi_ref, vec_ref, o_ref, h_ref):
    step = pl.program_id(1)

    xb = x_ref[...]
    x = xb.astype(F32)
    ts = x.shape[0]
    r = _sigmoid(_dot(xb, wr_ref[...]) + vec_ref[0:1, :])
    gate_i = _sigmoid(_dot(xb, wi_ref[...]) + vec_ref[1:2, :])
    log_a = -LRU_C * r * _softplus(-vec_ref[2:3, :])
    a = jnp.exp(log_a)
    one_minus_a2 = -jnp.tanh(log_a) * (a * a + 1.0)
    mult = one_minus_a2 * lax.rsqrt(jnp.maximum(one_minus_a2, 1e-30))
    row = lax.broadcasted_iota(jnp.int32, (ts, LRU_WIDTH), 0)
    mult = jnp.where(row + step * ts == 0, 1.0, mult)
    b = mult * gate_i * x

    row_in = row % SUBLANES
    shift = 1
    while shift < SUBLANES:
        keep = row_in >= shift
        a_prev = jnp.where(keep, pltpu.roll(a, shift, 0), 1.0)
        b_prev = jnp.where(keep, pltpu.roll(b, shift, 0), 0.0)
        b = a * b_prev + b
        a = a * a_prev
        shift *= 2
    carry = h_ref[...]
    blocks = []
    for blk in range(ts // SUBLANES):
        rows = slice(blk * SUBLANES, (blk + 1) * SUBLANES)
        blocks.append(b[rows, :] + a[rows, :] * carry)
        carry = blocks[-1][SUBLANES - 1:SUBLANES, :]
    h_ref[...] = carry
    o_ref[...] = (jnp.concatenate(blocks, axis=0) * g_ref[...].astype(F32)).astype(BF16)


def _lru_mixer(lam, wr, br, wi, bi):
    eye = jnp.eye(LRU_BLOCKS, dtype=F32)

    def blockdiag(w):
        return jnp.einsum('nde,nm->ndme', w, eye).reshape(LRU_WIDTH, LRU_WIDTH).astype(BF16)

    consts = [blockdiag(wr), blockdiag(wi), jnp.stack([br, bi, lam])]
    cols = [(COL_XLRU, LRU_WIDTH), (COL_GLRU, LRU_WIDTH)]
    return dict(body=_lru_kernel, cols=cols, small=False, consts=consts, width=LRU_WIDTH,
                scratch=pltpu.VMEM((1, LRU_WIDTH), F32))


def _mixers_kernel(*refs, layout):
    n_in = sum(n for _, n in layout)
    nmix = len(layout)
    gm_ref, x_ref, mod_ref = refs[n_in:n_in + 3]
    w_refs = refs[n_in + 3:n_in + 3 + nmix]
    wo_ref, o_ref = refs[n_in + 3 + nmix:n_in + 5 + nmix]
    states = refs[n_in + 5 + nmix:n_in + 5 + 2 * nmix]
    ybufs = refs[n_in + 5 + 2 * nmix:]

    @pl.when(pl.program_id(1) == 0)
    def _():
        for st in states:
            st[...] = jnp.zeros_like(st)

    pos = 0
    merged = None
    for i, ((body, n), ybuf, st, w_ref) in enumerate(zip(layout, ybufs, states, w_refs)):
        body(*refs[pos:pos + n], ybuf, st)
        pos += n
        term = gm_ref[:, i * D_MODEL:(i + 1) * D_MODEL].astype(F32) * _dot(ybuf[...], w_ref[...])
        merged = term if merged is None else merged + term
    out = _dot(merged.astype(BF16), wo_ref[...])
    o_ref[...] = x_ref[...] + mod_ref[:, 2 * D_MODEL:3 * D_MODEL] * out


def _mixers_merge(p_main, small, x2, mod3, w_branch, w_out, bsz, seq, parts):
    ts = min(MIX_TILE, seq)
    nsb = seq // ts

    def rows(b, s):
        return b * nsb + s

    operands, in_specs, layout = [], [], []
    for part in parts:
        for off, width in part["cols"]:
            operands.append(p_main)
            in_specs.append(pl.BlockSpec((ts, width), functools.partial(
                lambda b, s, col: (rows(b, s), col), col=off // width)))
        if part["small"]:
            operands.append(small)
            in_specs.append(pl.BlockSpec((ts, SMALL_COLS), lambda b, s: (rows(b, s), 0)))
        operands += part["consts"]
        in_specs += [_full(a.shape) for a in part["consts"]]
        layout.append((part["body"], len(part["cols"]) + int(part["small"]) + len(part["consts"])))
    gw = N_BRANCH * D_MODEL
    operands += [p_main, x2, mod3]
    in_specs += [
        pl.BlockSpec((ts, gw), lambda b, s: (rows(b, s), COL_GMERGE // gw)),
        pl.BlockSpec((ts, D_MODEL), lambda b, s: (rows(b, s), 0)),
        pl.BlockSpec((None, 1, 6 * D_MODEL), lambda b, s: (b, 0, 0)),
    ]
    acc = 0
    for part in parts:
        operands.append(w_branch[acc:acc + part["width"]].astype(BF16))
        in_specs.append(_full((part["width"], D_MODEL)))
        acc += part["width"]
    operands.append(w_out.astype(BF16))
    in_specs.append(_full((D_MODEL, D_MODEL)))
    return pl.pallas_call(
        functools.partial(_mixers_kernel, layout=tuple(layout)),
        grid=(bsz, nsb),
        in_specs=in_specs,
        out_specs=pl.BlockSpec((ts, D_MODEL), lambda b, s: (rows(b, s), 0)),
        out_shape=jax.ShapeDtypeStruct((bsz * seq, D_MODEL), F32),
        scratch_shapes=[part["scratch"] for part in parts]
        + [pltpu.VMEM((ts, part["width"]), BF16) for part in parts],
        compiler_params=_params("parallel", "arbitrary"),
        name="mixers_merge",
    )(*operands)


FFN_SLICES = tuple((s, min(512, FFN_HIDDEN - s)) for s in range(0, FFN_HIDDEN, 512))


def _ffn_kernel(x_ref, mod_ref, g_ref, w13_ref, w2_ref, gf_ref, o_ref, *, final):
    x = x_ref[...]
    h = _modulated_norm(x, g_ref[...], mod_ref[:, 3 * D_MODEL:4 * D_MODEL], mod_ref[:, 4 * D_MODEL:5 * D_MODEL])
    hb = h.astype(BF16)
    acc = None
    for start, size in FFN_SLICES:
        a = _dot(hb, w13_ref[:, start:start + size])
        b = _dot(hb, w13_ref[:, FFN_HIDDEN + start:FFN_HIDDEN + start + size])
        part = _dot((_silu(a) * b).astype(BF16), w2_ref[start:start + size, :])
        acc = part if acc is None else acc + part
    x = x + mod_ref[:, 5 * D_MODEL:6 * D_MODEL] * acc
    if final:
        x = x * lax.rsqrt(jnp.mean(x * x, axis=-1, keepdims=True) + NORM_EPS) * gf_ref[...]
    o_ref[...] = x


def _ffn(x2, mod3, ln_g, w13, w2, ln_final, seq, layer, final):
    t = x2.shape[0]
    tm = min(512, seq)
    per_b = seq // tm
    return pl.pallas_call(
        functools.partial(_ffn_kernel, final=final),
        grid=(t // tm,),
        in_specs=[
            pl.BlockSpec((tm, D_MODEL), lambda i: (i, 0)),
            pl.BlockSpec((None, 1, 6 * D_MODEL), lambda i: (i // per_b, 0, 0)),
            _full((1, D_MODEL)),
            pl.BlockSpec((None, D_MODEL, 2 * FFN_HIDDEN), lambda i: (layer, 0, 0), pipeline_mode=pl.Buffered(1)),
            pl.BlockSpec((None, FFN_HIDDEN, D_MODEL), lambda i: (layer, 0, 0), pipeline_mode=pl.Buffered(1)),
            _full((1, D_MODEL)),
        ],
        out_specs=pl.BlockSpec((tm, D_MODEL), lambda i: (i, 0)),
        out_shape=jax.ShapeDtypeStruct((t, D_MODEL), F32),
        compiler_params=_params("parallel"),
        name="ffn_final" if final else "ffn",
    )(x2, mod3, ln_g, w13, w2, ln_final)


def _split_in_weights(w_in, conv_w, conv_b):
    o = 0
    seg = {}
    for name, width in (("q", GDN_QK), ("k", GDN_QK), ("v", GDN_WIDTH), ("xssd", SSD_WIDTH), ("bssd", SSD_BC),
                        ("cssd", SSD_BC), ("xlru", LRU_WIDTH), ("us5", S5_WIDTH), ("bgdn", GDN_HEADS),
                        ("agdn", GDN_HEADS), ("zgdn", GDN_WIDTH), ("zssd", SSD_WIDTH), ("dtssd", SSD_HEADS),
                        ("glru", LRU_WIDTH), ("gmerge", N_BRANCH * D_MODEL)):
        seg[name] = (o, o + width)
        o += width

    def cols(arr, names):
        return jnp.concatenate([arr[..., seg[n][0]:seg[n][1]] for n in names], axis=-1)

    conv_names = ("q", "k", "v", "xssd", "xlru", "bssd", "cssd")
    wb = w_in.astype(BF16)
    def piece(name, lo=None, hi=None):
        start = seg[name][0]
        return wb[..., start + (lo or 0):(start + hi) if hi is not None else seg[name][1]]

    small_names = ("bgdn", "agdn", "dtssd")
    used = sum(seg[n][1] - seg[n][0] for n in small_names)
    pieces = ([piece(n) for n in conv_names] + [piece("us5", hi=2 * LANES)]
              + [piece(n) for n in ("zgdn", "zssd", "gmerge", "glru")] + [piece("us5", lo=2 * LANES)]
              + [piece(n) for n in small_names] + [jnp.zeros(wb.shape[:-1] + (SMALL_COLS - used,), BF16)])
    return (jnp.concatenate(pieces, axis=-1), cols(conv_w, conv_names),
            cols(conv_b[:, None, :], conv_names))


def kernel(x, c, ln_mix_g, ln_ffn_g, ln_final_g, ada_w, ada_b, w_in, conv_w, conv_b, s5_lambda_re, s5_lambda_im, s5_log_dt, s5_b_re, s5_b_im, s5_c_re, s5_c_im, s5_d, s5_glu_w, s5_glu_b, gdn_a_log, gdn_dt_bias, gdn_norm_w, ssd_a_log, ssd_dt_bias, ssd_d, ssd_norm_w, lru_lambda, lru_wr, lru_br, lru_wi, lru_bi, w_branch, w_out, ffn_w13, ffn_w2):
    bsz, seq, d = x.shape
    t = bsz * seq
    x2 = x.reshape(t, d)
    mod = _ada_mod(c, ada_w, ada_b)
    ln_final = ln_final_g.reshape(1, d)
    w_all, cw, cb = _split_in_weights(w_in, conv_w, conv_b)
    w13, w2 = ffn_w13.astype(BF16), ffn_w2.astype(BF16)
    for l in range(DEPTH):
        mod3 = mod[l].reshape(bsz, 1, 6 * d)
        p_main, small = _inproj(x2, mod3, ln_mix_g[l].reshape(1, d), w_all, cw, cb, seq, l)

        tables = _s5_tables(s5_lambda_re[l], s5_lambda_im[l], s5_log_dt[l], s5_b_re[l], s5_b_im[l],
                            s5_c_re[l], s5_c_im[l], min(CHUNK, seq))
        mix_ts = min(MIX_TILE, seq)
        x2 = _mixers_merge(p_main, small, x2, mod3, w_branch[l], w_out[l], bsz, seq, (
            _s5_mixer(tables, s5_d[l], s5_glu_w[l], s5_glu_b[l]),
            _gdn_mixer(mix_ts, gdn_a_log[l], gdn_dt_bias[l], gdn_norm_w[l]),
            _ssd_mixer(mix_ts, ssd_a_log[l], ssd_dt_bias[l], ssd_d[l], ssd_norm_w[l]),
            _lru_mixer(lru_lambda[l], lru_wr[l], lru_br[l], lru_wi[l], lru_bi[l])))
        x2 = _ffn(x2, mod3, ln_ffn_g[l].reshape(1, d), w13, w2, ln_final, seq, l, final=(l == DEPTH - 1))
    return x2.reshape(bsz, seq, d)
```
